```python
import math
import jax
import jax.numpy as jnp
from jax import lax
import numpy as np

D_MODEL = 2048
BATCH = 2
SEQ = 16384
DEPTH = 4
DEC_BATCH = 4
DEC_SEQ = 2048
PAST_LEN = 128

N_MIXERS = 3
GRID_W = 64
RMS_EPS = 1e-6
NEG_INF = -1e30
HEAD_DIM = 128

A_GROUPS = 4
A_WIDTH = D_MODEL
A_GROUP_CH = A_WIDTH // A_GROUPS

B_PAIRS = ((128, 1), (512, 4), (2048, 16))
B_N_GROUPS = len(B_PAIRS)
B_HEADS = 8
B_WIDTH = B_HEADS * HEAD_DIM
T5_BUCKETS = 32
T5_MAX_DIST = 1024

C_HEADS = 16
C_WIDTH = C_HEADS * HEAD_DIM
NA_ROWS = 8
NA_COLS = 16
NA_COL_BLOCK = 16
NA_KEY_COLS = NA_COL_BLOCK + NA_COLS

N_EXPERTS = 16
D_EXPERT = 4096
EC_CAPACITY = 2

kernel_name = 'hybrid_fnet_dilated_natten_ec_encoder'


def _rmsnorm(x, g):
    xf = x.astype(jnp.float32)
    y = xf * lax.rsqrt(jnp.mean(xf * xf, axis=-1, keepdims=True) + RMS_EPS)
    return y.astype(x.dtype) * g


def _t5_bucket(rel):
    nb = T5_BUCKETS // 2
    max_exact = nb // 2
    n = np.abs(rel)
    large = max_exact + (np.log(np.maximum(n, 1) / max_exact) / math.log(T5_MAX_DIST / max_exact)
                         * (nb - max_exact)).astype(np.int32)
    large = np.minimum(large, nb - 1)
    return (np.where(rel > 0, nb, 0) + np.where(n < max_exact, n, large)).astype(np.int32)


def _fourier_mixer(h, w_in, w_out):
    bsz, seq, _ = h.shape
    u = (h @ w_in).astype(jnp.float32).reshape(bsz, seq, A_GROUPS, A_GROUP_CH)
    f = jnp.fft.fftn(u, axes=(1, 3), norm='ortho').real
    return f.reshape(bsz, seq, A_WIDTH).astype(h.dtype) @ w_out


def _dilated_window_attention(q, k, v, bias_table, window, dilation):
    bsz, seq, nh, hd = q.shape
    half = window // (2 * dilation)
    sub_len = seq // dilation
    nblk = -(-sub_len // half)
    pad = nblk * half - sub_len

    def to_sub(t):
        return t.reshape(bsz, sub_len, dilation, nh, hd).transpose(0, 2, 1, 3, 4)

    qs = jnp.pad(to_sub(q), ((0, 0), (0, 0), (0, pad), (0, 0), (0, 0)))
    qs = qs.reshape(bsz, dilation, nblk, half, nh, hd)

    def key_blocks(t):
        ts = jnp.pad(to_sub(t), ((0, 0), (0, 0), (half, pad + half), (0, 0), (0, 0)))
        ts = ts.reshape(bsz, dilation, nblk + 2, half, nh, hd)
        return jnp.concatenate([ts[:, :, :-2], ts[:, :, 1:-1], ts[:, :, 2:]], axis=3)

    kw = key_blocks(k)
    vw = key_blocks(v)
    off = np.arange(3 * half)[None, :] - half - np.arange(half)[:, None]
    band = np.abs(off) <= half
    kpos = np.arange(nblk)[:, None] * half + np.arange(3 * half)[None, :] - half
    mask = band[None] & ((kpos >= 0) & (kpos < sub_len))[:, None, :]
    bias = jnp.take(bias_table, _t5_bucket(off * dilation), axis=0).transpose(2, 0, 1)
    s = jnp.einsum('brnqhd,brnkhd->brnhqk', qs, kw, preferred_element_type=jnp.float32)
    s = s * (hd ** -0.5) + bias[None, None, None].astype(jnp.float32)
    s = jnp.where(mask[None, None, :, None], s, NEG_INF)
    m = jnp.max(s, axis=-1, keepdims=True)
    p = jnp.exp(s - m)
    den = jnp.sum(p, axis=-1, keepdims=True)
    o = jnp.einsum('brnhqk,brnkhd->brnqhd', (p / den).astype(v.dtype), vw)
    lse = (m + jnp.log(den))[..., 0]

    def from_sub(t):
        t = t[:, :, :sub_len]
        return jnp.swapaxes(t, 1, 2).reshape((bsz, seq) + t.shape[3:])

    o = from_sub(o.reshape(bsz, dilation, nblk * half, nh, hd))
    lse = from_sub(lse.transpose(0, 1, 2, 4, 3).reshape(bsz, dilation, nblk * half, nh))
    return o, lse


def _dilated_mixer(h, w_qkv, w_out, t5_bias):
    bsz, seq, _ = h.shape
    qkv = (h @ w_qkv).reshape(bsz, seq, B_N_GROUPS, 3, B_HEADS, HEAD_DIM)
    outs = []
    lses = []
    for g, (window, dilation) in enumerate(B_PAIRS):
        o, l = _dilated_window_attention(qkv[:, :, g, 0], qkv[:, :, g, 1], qkv[:, :, g, 2],
                                         t5_bias[:, g * B_HEADS:(g + 1) * B_HEADS], window, dilation)
        outs.append(o)
        lses.append(l)
    alpha = jax.nn.softmax(jnp.stack(lses, axis=0), axis=0)
    o = jnp.einsum('gbsh,gbshd->bshd', alpha, jnp.stack(outs, axis=0).astype(jnp.float32))
    return o.reshape(bsz, seq, B_WIDTH).astype(h.dtype) @ w_out


def _neighbourhood_mixer(h, w_qkv, w_out, rpb):
    bsz, seq, _ = h.shape
    rows = seq // GRID_W
    kr = min(NA_ROWS, rows)
    qkv = (h @ w_qkv).reshape(bsz, rows, GRID_W, 3, C_HEADS, HEAD_DIM)
    q, k, v = qkv[:, :, :, 0], qkv[:, :, :, 1], qkv[:, :, :, 2]
    n_cb = GRID_W // NA_COL_BLOCK
    qcol = np.arange(GRID_W).reshape(n_cb, NA_COL_BLOCK)
    kstart = np.clip(np.arange(n_cb) * NA_COL_BLOCK - NA_COLS // 2, 0, GRID_W - NA_KEY_COLS)
    kcol = kstart[:, None] + np.arange(NA_KEY_COLS)[None, :]
    cstart = np.clip(qcol - NA_COLS // 2, 0, GRID_W - NA_COLS)
    col_ok = (kcol[:, None, :] >= cstart[..., None]) & (kcol[:, None, :] < cstart[..., None] + NA_COLS)
    dcol = np.clip(kcol[:, None, :] - qcol[..., None], -(NA_COLS - 1), NA_COLS - 1) + NA_COLS - 1
    kg = jnp.take(k, kcol, axis=2)
    vg = jnp.take(v, kcol, axis=2)
    qb = q.reshape(bsz, rows, n_cb, NA_COL_BLOCK, C_HEADS, HEAD_DIM)
    scale = HEAD_DIM ** -0.5

    def one_row(r):
        rs = jnp.clip(r - kr // 2, 0, rows - kr)
        kk = lax.dynamic_slice_in_dim(kg, rs, kr, axis=1)
        vv = lax.dynamic_slice_in_dim(vg, rs, kr, axis=1)
        qq = lax.dynamic_index_in_dim(qb, r, axis=1, keepdims=False)
        s = jnp.einsum('bjqhd,bijkhd->bhjqik', qq, kk, preferred_element_type=jnp.float32) * scale
        drow = rs + jnp.arange(kr) - r + NA_ROWS - 1
        bias = jnp.take(jnp.take(rpb, drow, axis=1), dcol, axis=2)
        s = s + bias.transpose(0, 2, 3, 1, 4)[None].astype(jnp.float32)
        s = jnp.where(col_ok[None, None, :, :, None, :], s, NEG_INF)
        p = jax.nn.softmax(s.reshape(s.shape[:4] + (kr * NA_KEY_COLS,)), axis=-1).reshape(s.shape)
        return jnp.einsum('bhjqik,bijkhd->bjqhd', p.astype(vv.dtype), vv)

    o = lax.map(one_row, jnp.arange(rows))
    o = jnp.moveaxis(o, 0, 1).reshape(bsz, seq, C_WIDTH)
    return o @ w_out


def _expert_choice_ffn(h, w_router, w_gate, w_up, w_down):
    bsz, seq, dm = h.shape
    n_tok = bsz * seq
    cap = EC_CAPACITY * n_tok // N_EXPERTS
    xt = h.reshape(n_tok, dm)
    aff = jax.nn.softmax((xt @ w_router).astype(jnp.float32), axis=-1)
    gate, idx = lax.top_k(aff.T, cap)
    xe = jnp.take(xt, idx, axis=0)
    hid = jax.nn.silu(jnp.einsum('ecd,edf->ecf', xe, w_gate)) * jnp.einsum('ecd,edf->ecf', xe, w_up)
    ye = jnp.einsum('ecf,efd->ecd', hid, w_down) * gate[..., None].astype(h.dtype)
    y = jnp.zeros((n_tok, dm), h.dtype).at[idx.reshape(-1)].add(ye.reshape(-1, dm))
    return y.reshape(bsz, seq, dm)


def _trunk(x, norm_mix, norm_ffn, norm_final, a_w_in, a_w_out, b_w_qkv, b_w_out, t5_bias,
           c_w_qkv, c_w_out, c_rpb, moe_router, moe_w_gate, moe_w_up, moe_w_down):
    for i in range(DEPTH):
        kind = i % N_MIXERS
        j = i // N_MIXERS
        h = _rmsnorm(x, norm_mix[i])
        if kind == 0:
            x = x + _fourier_mixer(h, a_w_in[j], a_w_out[j])
        elif kind == 1:
            x = x + _dilated_mixer(h, b_w_qkv[j], b_w_out[j], t5_bias)
        else:
            x = x + _neighbourhood_mixer(h, c_w_qkv[j], c_w_out[j], c_rpb[j])
        x = x + _expert_choice_ffn(_rmsnorm(x, norm_ffn[i]), moe_router[i], moe_w_gate[i],
                                   moe_w_up[i], moe_w_down[i])
    return _rmsnorm(x, norm_final)


def setup_inputs(seed: int = 0) -> dict:
    key = jax.random.key(seed)
    ks = jax.random.split(key, 17)
    n_a = len(range(0, DEPTH, N_MIXERS))
    n_b = len(range(1, DEPTH, N_MIXERS))
    n_c = len(range(2, DEPTH, N_MIXERS))

    def nrm(k, shape, scale):
        return jax.random.normal(k, shape, jnp.float32) * scale

    return {
        'x_prompt': nrm(ks[0], (BATCH, SEQ, D_MODEL), 1.0),
        'x_sample': nrm(ks[1], (DEC_BATCH, DEC_SEQ, D_MODEL), 1.0),
        'norm_mix': 1.0 + nrm(ks[2], (DEPTH, D_MODEL), 0.02),
        'norm_ffn': 1.0 + nrm(ks[3], (DEPTH, D_MODEL), 0.02),
        'norm_final': 1.0 + nrm(ks[4], (D_MODEL,), 0.02),
        'a_w_in': nrm(ks[5], (n_a, D_MODEL, A_WIDTH), D_MODEL ** -0.5),
        'a_w_out': nrm(ks[6], (n_a, A_WIDTH, D_MODEL), A_WIDTH ** -0.5),
        'b_w_qkv': nrm(ks[7], (n_b, D_MODEL, B_N_GROUPS * 3 * B_WIDTH), D_MODEL ** -0.5),
        'b_w_out': nrm(ks[8], (n_b, B_WIDTH, D_MODEL), B_WIDTH ** -0.5),
        't5_bias': nrm(ks[9], (T5_BUCKETS, B_N_GROUPS * B_HEADS), 0.1),
        'c_w_qkv': nrm(ks[10], (n_c, D_MODEL, 3 * C_WIDTH), D_MODEL ** -0.5),
        'c_w_out': nrm(ks[11], (n_c, C_WIDTH, D_MODEL), C_WIDTH ** -0.5),
        'c_rpb': nrm(ks[12], (n_c, C_HEADS, 2 * NA_ROWS - 1, 2 * NA_COLS - 1), 0.1),
        'moe_router': nrm(ks[13], (DEPTH, D_MODEL, N_EXPERTS), D_MODEL ** -0.5),
        'moe_w_gate': nrm(ks[14], (DEPTH, N_EXPERTS, D_MODEL, D_EXPERT), D_MODEL ** -0.5),
        'moe_w_up': nrm(ks[15], (DEPTH, N_EXPERTS, D_MODEL, D_EXPERT), D_MODEL ** -0.5),
        'moe_w_down': nrm(ks[16], (DEPTH, N_EXPERTS, D_EXPERT, D_MODEL), D_EXPERT ** -0.5),
    }


def reference(x_prompt, x_sample, norm_mix, norm_ffn, norm_final, a_w_in, a_w_out, b_w_qkv,
              b_w_out, t5_bias, c_w_qkv, c_w_out, c_rpb, moe_router, moe_w_gate, moe_w_up,
              moe_w_down):
    y_prompt = _trunk(x_prompt, norm_mix, norm_ffn, norm_final, a_w_in, a_w_out, b_w_qkv, b_w_out,
                      t5_bias, c_w_qkv, c_w_out, c_rpb, moe_router, moe_w_gate, moe_w_up, moe_w_down)
    y_sample = _trunk(x_sample, norm_mix, norm_ffn, norm_final, a_w_in, a_w_out, b_w_qkv, b_w_out,
                      t5_bias, c_w_qkv, c_w_out, c_rpb, moe_router, moe_w_gate, moe_w_up, moe_w_down)
    return (y_prompt, y_sample)
```

```python
import functools
import math

import numpy as np
import jax
import jax.numpy as jnp
from jax import lax
from jax.experimental import pallas as pl
from jax.experimental.pallas import tpu as pltpu

F32 = jnp.float32
BF16 = jnp.bfloat16

RMS_EPS = 1e-6
NEG_INF = -1e30
HEAD_DIM = 128
GRID_W = 64
A_GROUPS = 4
B_PAIRS = ((128, 1), (512, 4), (2048, 16))
T5_BUCKETS = 32
T5_MAX_DIST = 1024
NA_ROWS = 8
NA_COLS = 16
EC_CAPACITY = 2

V7X_VMEM_BYTES = 64 * 1024 * 1024
VMEM_LIMIT = 56 * 1024 * 1024
LANES = 128


def _params(sem):
    return pltpu.CompilerParams(dimension_semantics=sem, vmem_limit_bytes=VMEM_LIMIT)


def _pick(n, pref):
    t = min(n, pref)
    while n % t:
        t //= 2
    return t


def _rmsnorm_rows(x, g):
    ms = jnp.mean(x * x, axis=-1, keepdims=True)
    return (x * lax.rsqrt(ms + RMS_EPS)) * g


def _norm_matmul_kernel(x_ref, g_ref, w_ref, o_ref, h_ref, *, rows):
    @pl.when(pl.program_id(1) == 0)
    def _():
        def body(c, carry):
            sl = pl.ds(pl.multiple_of(c * rows, rows), rows)
            h_ref[sl, :] = _rmsnorm_rows(x_ref[sl, :], g_ref[...]).astype(BF16)
            return carry
        lax.fori_loop(0, x_ref.shape[0] // rows, body, 0)

    o_ref[...] = jnp.dot(h_ref[...], w_ref[...], preferred_element_type=F32).astype(o_ref.dtype)


def _norm_matmul(x, g, w, *, tm=1024, tn=1024, out_dtype=BF16):
    t, d = x.shape
    n = w.shape[1]
    tm = _pick(t, tm)
    tn = _pick(n, tn)
    rows = _pick(tm, 256)
    return pl.pallas_call(
        functools.partial(_norm_matmul_kernel, rows=rows),
        grid=(t // tm, n // tn),
        in_specs=[pl.BlockSpec((tm, d), lambda i, j: (i, 0)),
                  pl.BlockSpec((1, d), lambda i, j: (0, 0)),
                  pl.BlockSpec((d, tn), lambda i, j: (0, j))],
        out_specs=pl.BlockSpec((tm, tn), lambda i, j: (i, j)),
        out_shape=jax.ShapeDtypeStruct((t, n), out_dtype),
        scratch_shapes=[pltpu.VMEM((tm, d), BF16)],
        compiler_params=_params(("parallel", "arbitrary")),
        name="norm_matmul",
    )(x, g.reshape(1, d), w)


def _matmul_res_kernel(a_ref, w_ref, x_ref, o_ref):
    o_ref[...] = x_ref[...] + jnp.dot(a_ref[...], w_ref[...], preferred_element_type=F32)


def _matmul_res(a, w, x, *, tm=512):
    t, k = a.shape
    d = w.shape[1]
    tm = _pick(t, tm)
    return pl.pallas_call(
        _matmul_res_kernel,
        grid=(t // tm,),
        in_specs=[pl.BlockSpec((tm, k), lambda i: (i, 0)),
                  pl.BlockSpec((k, d), lambda i: (0, 0)),
                  pl.BlockSpec((tm, d), lambda i: (i, 0))],
        out_specs=pl.BlockSpec((tm, d), lambda i: (i, 0)),
        out_shape=jax.ShapeDtypeStruct((t, d), F32),
        compiler_params=_params(("parallel",)),
        name="matmul_res",
    )(a, w, x)


def _rmsnorm_kernel(x_ref, g_ref, o_ref):
    o_ref[...] = _rmsnorm_rows(x_ref[...], g_ref[...])


def _rmsnorm(x, g, *, tm=256):
    t, d = x.shape
    tm = _pick(t, tm)
    return pl.pallas_call(
        _rmsnorm_kernel,
        grid=(t // tm,),
        in_specs=[pl.BlockSpec((tm, d), lambda i: (i, 0)),
                  pl.BlockSpec((1, d), lambda i: (0, 0))],
        out_specs=pl.BlockSpec((tm, d), lambda i: (i, 0)),
        out_shape=jax.ShapeDtypeStruct((t, d), F32),
        compiler_params=_params(("parallel",)),
        name="final_rmsnorm",
    )(x, g.reshape(1, d))


FFT_S2 = 128


def _dft_tables(seq, ch):
    s1n = seq // FFT_S2
    k1 = np.arange(s1n, dtype=np.float64)
    ang1 = 2.0 * np.pi * np.outer(k1, k1) / s1n
    m1 = np.concatenate([np.cos(ang1), -np.sin(ang1)], axis=0)
    s2 = np.arange(FFT_S2, dtype=np.float64)
    kk = k1[:, None, None] + s1n * s2[None, :, None]
    ang2 = 2.0 * np.pi * ((kk * s2[None, None, :]) % seq) / seq
    gc, gs = np.cos(ang2), np.sin(ang2)
    g = np.concatenate([np.concatenate([gc, gs], axis=2),
                        np.concatenate([-gs, gc], axis=2)], axis=1)
    c = np.arange(ch, dtype=np.float64)
    angc = 2.0 * np.pi * (np.outer(c, c) % ch) / ch
    cs = np.concatenate([np.cos(angc), np.sin(angc)], axis=0) / math.sqrt(seq * ch)
    return (jnp.asarray(m1, F32).astype(BF16), jnp.asarray(g, F32).astype(BF16),
            jnp.asarray(cs, F32).astype(BF16))


def _fft_stage1_kernel(m_ref, u_ref, y_ref):
    y_ref[...] = jnp.dot(m_ref[...], u_ref[...], preferred_element_type=F32).astype(y_ref.dtype)


def _fft_stage1(u, m1, *, tn=8192):
    b, s1n, n = u.shape
    tn = _pick(n, tn)
    return pl.pallas_call(
        _fft_stage1_kernel,
        grid=(b, n // tn),
        in_specs=[pl.BlockSpec((2 * s1n, s1n), lambda i, j: (0, 0)),
                  pl.BlockSpec((None, s1n, tn), lambda i, j: (i, 0, j))],
        out_specs=pl.BlockSpec((None, 2 * s1n, tn), lambda i, j: (i, 0, j)),
        out_shape=jax.ShapeDtypeStruct((b, 2 * s1n, n), BF16),
        compiler_params=_params(("parallel", "parallel")),
        name="fft_stage1",
    )(m1, u)


def _fft_stage2_kernel(g_ref, yr_ref, yi_ref, cs_ref, w_ref, x_ref, o_ref, f_ref, *, kb, groups):
    d = w_ref.shape[0]
    ch = d // groups
    for j in range(kb):
        y = jnp.concatenate([yr_ref[j], yi_ref[j]], axis=0)
        xc = jnp.dot(g_ref[j], y, preferred_element_type=F32).astype(BF16)
        for q in range(groups):
            xg = jnp.concatenate([xc[:FFT_S2, q * ch:(q + 1) * ch],
                                  xc[FFT_S2:, q * ch:(q + 1) * ch]], axis=1)
            f_ref[j * FFT_S2:(j + 1) * FFT_S2, q * ch:(q + 1) * ch] = jnp.dot(
                xg, cs_ref[...], preferred_element_type=F32).astype(BF16)
    r = jnp.dot(f_ref[...], w_ref[...], preferred_element_type=F32)
    for j in range(kb):
        o_ref[:, j * d:(j + 1) * d] = x_ref[:, j * d:(j + 1) * d] + r[j * FFT_S2:(j + 1) * FFT_S2]


def _fft_stage2(y, g, cs, w_out, x, *, kb=2):
    b, s1n2, _, d = y.shape
    s1n = s1n2 // 2
    kb = _pick(s1n, kb)
    nk = s1n // kb
    return pl.pallas_call(
        functools.partial(_fft_stage2_kernel, kb=kb, groups=A_GROUPS),
        grid=(b, nk),
        in_specs=[pl.BlockSpec((kb, 2 * FFT_S2, 2 * FFT_S2), lambda i, j: (j, 0, 0)),
                  pl.BlockSpec((None, kb, FFT_S2, d), lambda i, j: (i, j, 0, 0)),
                  pl.BlockSpec((None, kb, FFT_S2, d), lambda i, j: (i, j + nk, 0, 0)),
                  pl.BlockSpec(cs.shape, lambda i, j: (0, 0)),
                  pl.BlockSpec((d, d), lambda i, j: (0, 0)),
                  pl.BlockSpec((None, FFT_S2, kb * d), lambda i, j: (i, 0, j))],
        out_specs=pl.BlockSpec((None, FFT_S2, kb * d), lambda i, j: (i, 0, j)),
        out_shape=jax.ShapeDtypeStruct(x.shape, F32),
        scratch_shapes=[pltpu.VMEM((kb * FFT_S2, d), BF16)],
        compiler_params=_params(("parallel", "parallel")),
        name="fft_stage2",
    )(g, y, y, cs, w_out, x)


def _fourier_layer(x, g_norm, w_in, w_out):
    b, s, d = x.shape
    assert s % FFT_S2 == 0 and d % A_GROUPS == 0
    s1n = s // FFT_S2
    m1, g, cs = _dft_tables(s, d // A_GROUPS)
    u = _norm_matmul(x.reshape(b * s, d), g_norm, w_in)
    y = _fft_stage1(u.reshape(b, s1n, FFT_S2 * d), m1)
    out = _fft_stage2(y.reshape(b, 2 * s1n, FFT_S2, d), g, cs, w_out, x.reshape(b, FFT_S2, s1n * d))
    return out.reshape(b, s, d)


DIL_TQ = 128
DIL_HALF = 64


def _t5_bucket_np(rel):
    nb = T5_BUCKETS // 2
    max_exact = nb // 2
    n = np.abs(rel)
    large = max_exact + (np.log(np.maximum(n, 1) / max_exact) / math.log(T5_MAX_DIST / max_exact)
                         * (nb - max_exact)).astype(np.int32)
    large = np.minimum(large, nb - 1)
    return (np.where(rel > 0, nb, 0) + np.where(n < max_exact, n, large)).astype(np.int32)


def _dilated_bias(t5_bias_g, dilation):
    off = np.arange(DIL_TQ + 2 * DIL_HALF)[None, :] - DIL_HALF - np.arange(DIL_TQ)[:, None]
    band = np.abs(off) <= DIL_HALF
    bias = jnp.take(t5_bias_g, _t5_bucket_np(off * dilation), axis=0).transpose(2, 0, 1)
    return jnp.where(band[None], bias.astype(F32), NEG_INF)


def _dilated_attn_kernel(bias_ref, q_ref, kp_ref, kc_ref, kn_ref, vp_ref, vc_ref, vn_ref,
                         o_ref, l_ref, *, heads, sub_len):
    i = pl.program_id(2)
    nk = DIL_TQ + 2 * DIL_HALF
    kpos = i * DIL_TQ - DIL_HALF + lax.broadcasted_iota(jnp.int32, (1, nk), 1)
    valid = (kpos >= 0) & (kpos < sub_len)
    lane = lax.broadcasted_iota(jnp.int32, (DIL_TQ, LANES), 1)
    lse_all = jnp.zeros((DIL_TQ, LANES), F32)
    scale = HEAD_DIM ** -0.5
    for h in range(heads):
        hs = slice(h * HEAD_DIM, (h + 1) * HEAD_DIM)
        k_win = jnp.concatenate([kp_ref[DIL_TQ - DIL_HALF:, hs], kc_ref[:, hs], kn_ref[:DIL_HALF, hs]], axis=0)
        v_win = jnp.concatenate([vp_ref[DIL_TQ - DIL_HALF:, hs], vc_ref[:, hs], vn_ref[:DIL_HALF, hs]], axis=0)
        s = lax.dot_general(q_ref[:, hs], k_win, (((1,), (1,)), ((), ())), preferred_element_type=F32)
        s = jnp.where(valid, s * scale + bias_ref[h], NEG_INF)
        m = jnp.max(s, axis=-1, keepdims=True)
        p = jnp.exp(s - m)
        den = jnp.sum(p, axis=-1, keepdims=True)
        o = jnp.dot((p * (1.0 / den)).astype(BF16), v_win, preferred_element_type=F32)
        o_ref[:, hs] = o.astype(o_ref.dtype)
        lse_all = jnp.where(lane == h, m + jnp.log(den), lse_all)
    l_ref[...] = lse_all


def _dilated_attn(qkv, bias, g, dilation, heads, n_groups):
    b, s, n = qkv.shape
    hw = heads * HEAD_DIM
    sub_len = s // dilation
    assert s % dilation == 0 and sub_len % DIL_TQ == 0
    nb = sub_len // DIL_TQ
    ncol = n // hw
    qkv_v = qkv.reshape(b, sub_len, dilation * n)

    def spec(c, shift):
        def imap(bi, rho, i):
            return (bi, jnp.clip(i + shift, 0, nb - 1), rho * ncol + g * 3 + c)
        return pl.BlockSpec((None, DIL_TQ, hw), imap)

    o, lse = pl.pallas_call(
        functools.partial(_dilated_attn_kernel, heads=heads, sub_len=sub_len),
        grid=(b, dilation, nb),
        in_specs=[pl.BlockSpec(bias.shape, lambda bi, rho, i: (0, 0, 0)),
                  spec(0, 0), spec(1, -1), spec(1, 0), spec(1, 1), spec(2, -1), spec(2, 0), spec(2, 1)],
        out_specs=[pl.BlockSpec((None, DIL_TQ, hw), lambda bi, rho, i: (bi, i, rho)),
                   pl.BlockSpec((None, DIL_TQ, LANES), lambda bi, rho, i: (bi, i, rho))],
        out_shape=[jax.ShapeDtypeStruct((b, sub_len, dilation * hw), BF16),
                   jax.ShapeDtypeStruct((b, sub_len, dilation * LANES), F32)],
        compiler_params=_params(("parallel", "parallel", "parallel")),
        name=f"dilated_attn_r{dilation}",
    )(bias, qkv_v, qkv_v, qkv_v, qkv_v, qkv_v, qkv_v, qkv_v)
    return o.reshape(b * s, hw), lse.reshape(b * s, LANES)


def _dilated_combine_kernel(o0_ref, o1_ref, o2_ref, l0_ref, l1_ref, l2_ref, w_ref, x_ref, out_ref, a_ref, *, heads):
    l0, l1, l2 = l0_ref[...], l1_ref[...], l2_ref[...]
    m = jnp.maximum(jnp.maximum(l0, l1), l2)
    e0, e1, e2 = jnp.exp(l0 - m), jnp.exp(l1 - m), jnp.exp(l2 - m)
    inv = 1.0 / (e0 + e1 + e2)
    a0, a1, a2 = e0 * inv, e1 * inv, e2 * inv
    for h in range(heads):
        hs = slice(h * HEAD_DIM, (h + 1) * HEAD_DIM)
        o = (a0[:, h:h + 1] * o0_ref[:, hs].astype(F32) + a1[:, h:h + 1] * o1_ref[:, hs].astype(F32)
             + a2[:, h:h + 1] * o2_ref[:, hs].astype(F32))
        a_ref[:, hs] = o.astype(BF16)
    out_ref[...] = x_ref[...] + jnp.dot(a_ref[...], w_ref[...], preferred_element_type=F32)


def _dilated_combine(os, ls, w_out, x, *, tm=512):
    t, d = x.shape
    hw = w_out.shape[0]
    tm = _pick(t, tm)
    row = lambda w: pl.BlockSpec((tm, w), lambda i: (i, 0))
    return pl.pallas_call(
        functools.partial(_dilated_combine_kernel, heads=hw // HEAD_DIM),
        grid=(t // tm,),
        in_specs=[row(hw), row(hw), row(hw), row(LANES), row(LANES), row(LANES),
                  pl.BlockSpec((hw, d), lambda i: (0, 0)), row(d)],
        out_specs=row(d),
        out_shape=jax.ShapeDtypeStruct((t, d), F32),
        scratch_shapes=[pltpu.VMEM((tm, hw), BF16)],
        compiler_params=_params(("parallel",)),
        name="dilated_combine",
    )(*os, *ls, w_out, x)


def _dilated_layer(x, g_norm, w_qkv, w_out, t5_bias):
    b, s, d = x.shape
    hw = w_out.shape[0]
    heads = hw // HEAD_DIM
    n_groups = len(B_PAIRS)
    assert n_groups == 3 and w_qkv.shape[1] == n_groups * 3 * hw
    qkv = _norm_matmul(x.reshape(b * s, d), g_norm, w_qkv).reshape(b, s, -1)
    os, ls = [], []
    for g, (window, dilation) in enumerate(B_PAIRS):
        assert window // (2 * dilation) == DIL_HALF
        bias = _dilated_bias(t5_bias[:, g * heads:(g + 1) * heads], dilation)
        o, l = _dilated_attn(qkv, bias, g, dilation, heads, n_groups)
        os.append(o)
        ls.append(l)
    return _dilated_combine(os, ls, w_out, x.reshape(b * s, d)).reshape(b, s, d)


NA_QROWS = 8
NA_KBLK = 4
NA_WIN = NA_QROWS + NA_ROWS


def _na_bias_table(rpb):
    scale = HEAD_DIM ** -0.5
    qc = np.arange(GRID_W)[None, :]
    kc = np.arange(GRID_W)[:, None]
    cstart = np.clip(qc - NA_COLS // 2, 0, GRID_W - NA_COLS)
    ok = (kc >= cstart) & (kc < cstart + NA_COLS)
    dcol = np.clip(kc - qc, -(NA_COLS - 1), NA_COLS - 1) + NA_COLS - 1
    tab = jnp.take(rpb.astype(F32), dcol, axis=2)
    tab = jnp.where(ok[None, None], tab, NEG_INF) / scale
    hi = tab.astype(BF16)
    lo = (tab - hi.astype(F32)).astype(BF16)
    out = jnp.concatenate([hi, lo], axis=-1)
    return out.reshape(rpb.shape[0], (2 * NA_ROWS - 1) * GRID_W, 2 * GRID_W)


def _na_attn_kernel(bk_ref, q_ref, k0_ref, k1_ref, k2_ref, k3_ref, v0_ref, v1_ref, v2_ref, v3_ref,
                    o_ref, kw_ref, vw_ref, *, heads, rows):
    rb = pl.program_id(1)
    blk = NA_KBLK * GRID_W
    for j, (kr, vr) in enumerate(((k0_ref, v0_ref), (k1_ref, v1_ref), (k2_ref, v2_ref), (k3_ref, v3_ref))):
        kw_ref[j * blk:(j + 1) * blk, :] = kr[...]
        vw_ref[j * blk:(j + 1) * blk, :] = vr[...]
    ws = NA_KBLK * jnp.clip(2 * rb - 1, 0, rows // NA_KBLK - NA_WIN // NA_KBLK)
    nkeys = NA_ROWS * GRID_W
    eye = (lax.broadcasted_iota(jnp.int32, (GRID_W, 2 * GRID_W), 0)
           == lax.broadcasted_iota(jnp.int32, (GRID_W, 2 * GRID_W), 1) % GRID_W)
    eye2 = jnp.where(eye, 1.0, 0.0).astype(BF16)
    scale = HEAD_DIM ** -0.5

    def row_body(rr, carry):
        r = rb * NA_QROWS + rr
        rs = jnp.clip(r - NA_ROWS // 2, 0, rows - NA_ROWS)
        koff = pl.multiple_of((rs - ws) * GRID_W, GRID_W)
        boff = pl.multiple_of((rs - r + NA_ROWS - 1) * GRID_W, GRID_W)
        qsl = pl.ds(pl.multiple_of(rr * GRID_W, GRID_W), GRID_W)
        for h in range(heads):
            hs = slice(h * HEAD_DIM, (h + 1) * HEAD_DIM)
            q_aug = jnp.concatenate([q_ref[qsl, hs], eye2], axis=1)
            k_aug = jnp.concatenate([kw_ref[pl.ds(koff, nkeys), hs], bk_ref[h, pl.ds(boff, nkeys), :]], axis=1)
            s = lax.dot_general(q_aug, k_aug, (((1,), (1,)), ((), ())), preferred_element_type=F32) * scale
            m = jnp.max(s, axis=-1, keepdims=True)
            p = jnp.exp(s - m)
            den = jnp.sum(p, axis=-1, keepdims=True)
            o = jnp.dot((p * (1.0 / den)).astype(BF16), vw_ref[pl.ds(koff, nkeys), hs], preferred_element_type=F32)
            o_ref[qsl, hs] = o.astype(o_ref.dtype)
        return carry

    lax.fori_loop(0, NA_QROWS, row_body, 0)


def _na_attn(qkv, bk, heads):
    b, s, _ = qkv.shape
    hw = heads * HEAD_DIM
    rows = s // GRID_W
    assert s % GRID_W == 0 and rows % NA_QROWS == 0 and rows >= NA_WIN and 2 * GRID_W == LANES
    nwb = rows // NA_KBLK - NA_WIN // NA_KBLK
    blk = NA_KBLK * GRID_W

    def kv_spec(c, j):
        return pl.BlockSpec((None, blk, hw), lambda bi, rb: (bi, jnp.clip(2 * rb - 1, 0, nwb) + j, c))

    o = pl.pallas_call(
        functools.partial(_na_attn_kernel, heads=heads, rows=rows),
        grid=(b, rows // NA_QROWS),
        in_specs=[pl.BlockSpec(bk.shape, lambda bi, rb: (0, 0, 0)),
                  pl.BlockSpec((None, NA_QROWS * GRID_W, hw), lambda bi, rb: (bi, rb, 0))]
                 + [kv_spec(1, j) for j in range(4)] + [kv_spec(2, j) for j in range(4)],
        out_specs=pl.BlockSpec((None, NA_QROWS * GRID_W, hw), lambda bi, rb: (bi, rb, 0)),
        out_shape=jax.ShapeDtypeStruct((b, s, hw), BF16),
        scratch_shapes=[pltpu.VMEM((NA_WIN * GRID_W, hw), BF16), pltpu.VMEM((NA_WIN * GRID_W, hw), BF16)],
        compiler_params=_params(("parallel", "parallel")),
        name="na_attn",
    )(bk, *([qkv] * 9))
    return o.reshape(b * s, hw)


def _neighbourhood_layer(x, g_norm, w_qkv, w_out, rpb):
    b, s, d = x.shape
    hw = w_out.shape[0]
    assert w_qkv.shape[1] == 3 * hw
    qkv = _norm_matmul(x.reshape(b * s, d), g_norm, w_qkv).reshape(b, s, 3 * hw)
    o = _na_attn(qkv, _na_bias_table(rpb), hw // HEAD_DIM)
    return _matmul_res(o, w_out, x.reshape(b * s, d)).reshape(b, s, d)


def _router_kernel(x_ref, g_ref, wt_ref, aff_ref):
    hn = _rmsnorm_rows(x_ref[...], g_ref[...])
    logits = lax.dot_general(wt_ref[...], hn, (((1,), (1,)), ((), ())),
                             precision=lax.Precision.HIGHEST, preferred_element_type=F32)
    m = jnp.max(logits, axis=0, keepdims=True)
    p = jnp.exp(logits - m)
    aff_ref[...] = p / jnp.sum(p, axis=0, keepdims=True)


def _router(x, g, w_router_t, *, tm=512):
    t, d = x.shape
    e = w_router_t.shape[0]
    tm = _pick(t, tm)
    return pl.pallas_call(
        _router_kernel,
        grid=(t // tm,),
        in_specs=[pl.BlockSpec((tm, d), lambda i: (i, 0)),
                  pl.BlockSpec((1, d), lambda i: (0, 0)),
                  pl.BlockSpec((e, d), lambda i: (0, 0))],
        out_specs=pl.BlockSpec((e, tm), lambda i: (0, i)),
        out_shape=jax.ShapeDtypeStruct((e, t), F32),
        compiler_params=_params(("parallel",)),
        name="moe_router",
    )(x, g.reshape(1, d), w_router_t)


def _topk_mask_kernel(aff_ref, sel_ref, *, cap):
    bits = pltpu.bitcast(aff_ref[...], jnp.int32)
    e, t = bits.shape

    def count(mask):
        return jnp.sum(jnp.where(mask, 1, 0), axis=1, keepdims=True)

    def value_step(_, c):
        lo, hi = c
        mid = lo + (hi - lo) // 2
        ok = count(bits >= mid) >= cap
        return jnp.where(ok, mid, lo), jnp.where(ok, hi, mid)

    lo0 = jnp.zeros((e, 1), jnp.int32)
    hi0 = jnp.full((e, 1), 0x7F800000, jnp.int32)
    thr, _ = lax.fori_loop(0, 31, value_step, (lo0, hi0))
    above = bits > thr
    tie = bits == thr
    need = cap - count(above)
    tok = lax.broadcasted_iota(jnp.int32, (e, t), 1)

    def index_step(_, c):
        lo, hi = c
        mid = lo + (hi - lo) // 2
        ok = count(tie & (tok < mid)) >= need
        return jnp.where(ok, lo, mid), jnp.where(ok, mid, hi)

    _, bound = lax.fori_loop(0, max(1, math.ceil(math.log2(t))), index_step,
                             (jnp.zeros((e, 1), jnp.int32), jnp.full((e, 1), t, jnp.int32)))
    sel_ref[...] = jnp.where(above | (tie & (tok < bound)), 1, 0)


def _topk_mask(aff_t, cap):
    e, t = aff_t.shape
    return pl.pallas_call(
        functools.partial(_topk_mask_kernel, cap=cap),
        grid=(1,),
        in_specs=[pl.BlockSpec((e, t), lambda i: (0, 0))],
        out_specs=pl.BlockSpec((e, t), lambda i: (0, 0)),
        out_shape=jax.ShapeDtypeStruct((e, t), jnp.int32),
        compiler_params=_params(("arbitrary",)),
        name="moe_topk_mask",
    )(aff_t)


def _moe_ffn_kernel(idx_ref, gate_ref, gn_ref, wg_ref, wu_ref, wd_ref, x_hbm, yin_hbm, out_hbm,
                    rows_ref, h_ref, acc_ref, sem, *, tm, nt, nf):
    del yin_hbm
    base = (pl.program_id(0) * nt + pl.program_id(1)) * tm
    f = pl.program_id(2)

    def row_copy(hbm, r, to_vmem):
        tok = idx_ref[base + r]
        src, dst = hbm.at[pl.ds(tok, 1), :], rows_ref.at[pl.ds(r, 1), :]
        return pltpu.make_async_copy(src, dst, sem.at[0]) if to_vmem else pltpu.make_async_copy(dst, src, sem.at[0])

    def all_rows(hbm, to_vmem):
        def start(r, c):
            row_copy(hbm, r, to_vmem).start()
            return c
        lax.fori_loop(0, tm, start, 0)

        def wait(r, c):
            row_copy(hbm, r, to_vmem).wait()
            return c
        lax.fori_loop(0, tm, wait, 0)

    @pl.when(f == 0)
    def _():
        all_rows(x_hbm, True)
        h_ref[...] = _rmsnorm_rows(rows_ref[...], gn_ref[...]).astype(BF16)
        acc_ref[...] = jnp.zeros_like(acc_ref)

    h = h_ref[...]
    g = jnp.dot(h, wg_ref[...], preferred_element_type=F32)
    u = jnp.dot(h, wu_ref[...], preferred_element_type=F32)
    hid = (g * (1.0 / (1.0 + jnp.exp(-g))) * u).astype(BF16)
    acc_ref[...] += jnp.dot(hid, wd_ref[...], preferred_element_type=F32)

    @pl.when(f == nf - 1)
    def _():
        all_rows(out_hbm, True)
        rows_ref[...] = rows_ref[...] + acc_ref[...] * gate_ref[...]
        all_rows(out_hbm, False)


def _moe_ffn(x, idx, gate, g_norm, wg, wu, wd, *, tm=512, tf=512):
    t, d = x.shape
    e, cap = idx.shape
    fdim = wg.shape[2]
    tm = _pick(cap, tm)
    tf = _pick(fdim, tf)
    nt, nf = cap // tm, fdim // tf
    grid_spec = pltpu.PrefetchScalarGridSpec(
        num_scalar_prefetch=1,
        grid=(e, nt, nf),
        in_specs=[pl.BlockSpec((tm, 1), lambda ei, i, f, idx: (ei * nt + i, 0)),
                  pl.BlockSpec((1, d), lambda ei, i, f, idx: (0, 0)),
                  pl.BlockSpec((None, d, tf), lambda ei, i, f, idx: (ei, 0, f)),
                  pl.BlockSpec((None, d, tf), lambda ei, i, f, idx: (ei, 0, f)),
                  pl.BlockSpec((None, tf, d), lambda ei, i, f, idx: (ei, f, 0)),
                  pl.BlockSpec(memory_space=pl.ANY),
                  pl.BlockSpec(memory_space=pl.ANY)],
        out_specs=pl.BlockSpec(memory_space=pl.ANY),
        scratch_shapes=[pltpu.VMEM((tm, d), F32), pltpu.VMEM((tm, d), BF16), pltpu.VMEM((tm, d), F32),
                        pltpu.SemaphoreType.DMA((1,))],
    )
    return pl.pallas_call(
        functools.partial(_moe_ffn_kernel, tm=tm, nt=nt, nf=nf),
        grid_spec=grid_spec,
        out_shape=jax.ShapeDtypeStruct((t, d), F32),
        input_output_aliases={7: 0},
        compiler_params=_params(("arbitrary", "arbitrary", "arbitrary")),
        name="moe_ffn",
    )(idx.reshape(e * cap), gate.reshape(e * cap, 1), g_norm.reshape(1, d), wg, wu, wd, x, x)


def _moe_layer(x, g_norm, w_router, wg, wu, wd):
    b, s, d = x.shape
    t = b * s
    e = w_router.shape[1]
    cap = EC_CAPACITY * t // e
    xt = x.reshape(t, d)
    aff_t = _router(xt, g_norm, w_router.T)
    sel = _topk_mask(aff_t, cap)
    idx = jnp.argsort(1 - sel, axis=1, stable=True)[:, :cap].astype(jnp.int32)
    gate = jnp.take_along_axis(aff_t, idx, axis=1)
    return _moe_ffn(xt, idx, gate, g_norm, wg, wu, wd).reshape(b, s, d)


N_MIXERS = 3


def _trunk(x, w):
    for i in range(w["norm_mix"].shape[0]):
        kind, j = i % N_MIXERS, i // N_MIXERS
        if kind == 0:
            x = _fourier_layer(x, w["norm_mix"][i], w["a_w_in"][j], w["a_w_out"][j])
        elif kind == 1:
            x = _dilated_layer(x, w["norm_mix"][i], w["b_w_qkv"][j], w["b_w_out"][j], w["t5_bias"])
        else:
            x = _neighbourhood_layer(x, w["norm_mix"][i], w["c_w_qkv"][j], w["c_w_out"][j], w["c_rpb"][j])
        x = _moe_layer(x, w["norm_ffn"][i], w["moe_router"][i], w["moe_w_gate"][i], w["moe_w_up"][i],
                       w["moe_w_down"][i])
    b, s, d = x.shape
    return _rmsnorm(x.reshape(b * s, d), w["norm_final"]).reshape(b, s, d)


def kernel(x_prompt, x_sample, norm_mix, norm_ffn, norm_final, a_w_in, a_w_out, b_w_qkv, b_w_out, t5_bias,
           c_w_qkv, c_w_out, c_rpb, moe_router, moe_w_gate, moe_w_up, moe_w_down):
    bf = lambda a: a.astype(BF16)
    w = dict(norm_mix=norm_mix, norm_ffn=norm_ffn, norm_final=norm_final, a_w_in=bf(a_w_in), a_w_out=bf(a_w_out),
             b_w_qkv=bf(b_w_qkv), b_w_out=bf(b_w_out), t5_bias=t5_bias, c_w_qkv=bf(c_w_qkv), c_w_out=bf(c_w_out),
             c_rpb=c_rpb, moe_router=moe_router, moe_w_gate=bf(moe_w_gate), moe_w_up=bf(moe_w_up),
             moe_w_down=bf(moe_w_down))
    return _trunk(x_prompt, w), _trunk(x_sample, w)
```

```python
import functools
import math

import numpy as np
import jax
import jax.numpy as jnp
from jax import lax
from jax.experimental import pallas as pl
from jax.experimental.pallas import tpu as pltpu

F32 = jnp.float32
BF16 = jnp.bfloat16

RMS_EPS = 1e-6
NEG_INF = -1e30
HEAD_DIM = 128
GRID_W = 64
A_GROUPS = 4
B_PAIRS = ((128, 1), (512, 4), (2048, 16))
T5_BUCKETS = 32
T5_MAX_DIST = 1024
NA_ROWS = 8
NA_COLS = 16
EC_CAPACITY = 2
N_MIXERS = 3

V7X_VMEM_BYTES = 64 * 1024 * 1024
VMEM_LIMIT = V7X_VMEM_BYTES - 8 * 1024 * 1024
LANES = 128
BF16_SUBLANES = 16


def _params(sem):
    return pltpu.CompilerParams(dimension_semantics=sem, vmem_limit_bytes=VMEM_LIMIT)


def _pick(n, pref):
    t = min(n, pref)
    while n % t:
        t //= 2
    return t


def _rmsnorm_rows(x, g):
    ms = jnp.mean(x * x, axis=-1, keepdims=True)
    return (x * lax.rsqrt(ms + RMS_EPS)) * g


def _norm_matmul_kernel(x_ref, g_ref, w_ref, o_ref, h_ref, r_ref, *, rows, dilation):
    @pl.when(pl.program_id(1) == 0)
    def _():
        def body(c, carry):
            sl = pl.ds(pl.multiple_of(c * rows, rows), rows)
            h_ref[sl, :] = _rmsnorm_rows(x_ref[sl, :], g_ref[...]).astype(BF16)
            return carry
        lax.fori_loop(0, x_ref.shape[0] // rows, body, 0)

    res = jnp.dot(h_ref[...], w_ref[...], preferred_element_type=F32)
    if dilation == 1:
        o_ref[0] = res.astype(o_ref.dtype)
    else:
        chunks = [slice(c * LANES, (c + 1) * LANES) for c in range(r_ref.shape[0])]
        for c, cs in enumerate(chunks):
            r_ref[c] = res[:, cs]
        per = r_ref.shape[1] // dilation
        for rho in range(dilation):
            for c, cs in enumerate(chunks):
                o_ref[rho, :, cs] = r_ref[c, pl.ds(rho, per, stride=dilation), :].astype(o_ref.dtype)


def _norm_matmul(x, g, w, *, col0=0, n=None, dilation=1, tm=1024, tn=1024):
    b, s, d = x.shape
    n = w.shape[1] if n is None else n
    tm = _pick(s, tm)
    tn = _pick(n, tn)
    assert col0 % tn == 0 and tm % (dilation * BF16_SUBLANES) == 0
    rows = _pick(tm, 256)
    per_b = s // tm
    cb0 = col0 // tn
    out = pl.pallas_call(
        functools.partial(_norm_matmul_kernel, rows=rows, dilation=dilation),
        grid=(b * per_b, n // tn),
        in_specs=[pl.BlockSpec((None, tm, d), lambda i, j: (i // per_b, i % per_b, 0)),
                  pl.BlockSpec((1, d), lambda i, j: (0, 0)),
                  pl.BlockSpec((d, tn), lambda i, j: (0, cb0 + j))],
        out_specs=pl.BlockSpec((None, dilation, tm // dilation, tn), lambda i, j: (i // per_b, 0, i % per_b, j)),
        out_shape=jax.ShapeDtypeStruct((b, dilation, s // dilation, n), BF16),
        scratch_shapes=[pltpu.VMEM((tm, d), BF16),
                        pltpu.VMEM((tn // LANES, tm, LANES) if dilation > 1 else (1, 8, LANES), F32)],
        compiler_params=_params(("parallel", "arbitrary")),
        name="norm_matmul",
    )(x, g.reshape(1, d), w)
    return out


def _matmul_res_kernel(a_ref, w_ref, x_ref, o_ref):
    o_ref[...] = x_ref[...] + jnp.dot(a_ref[...], w_ref[...], preferred_element_type=F32)


def _matmul_res(a, w, x, *, tm=512):
    t, k = a.shape
    d = w.shape[1]
    tm = _pick(t, tm)
    return pl.pallas_call(
        _matmul_res_kernel,
        grid=(t // tm,),
        in_specs=[pl.BlockSpec((tm, k), lambda i: (i, 0)),
                  pl.BlockSpec((k, d), lambda i: (0, 0)),
                  pl.BlockSpec((tm, d), lambda i: (i, 0))],
        out_specs=pl.BlockSpec((tm, d), lambda i: (i, 0)),
        out_shape=jax.ShapeDtypeStruct((t, d), F32),
        compiler_params=_params(("parallel",)),
        name="matmul_res",
    )(a, w, x)


def _rmsnorm_kernel(x_ref, g_ref, o_ref):
    o_ref[...] = _rmsnorm_rows(x_ref[...], g_ref[...])


def _rmsnorm(x, g, *, tm=256):
    t, d = x.shape
    tm = _pick(t, tm)
    return pl.pallas_call(
        _rmsnorm_kernel,
        grid=(t // tm,),
        in_specs=[pl.BlockSpec((tm, d), lambda i: (i, 0)),
                  pl.BlockSpec((1, d), lambda i: (0, 0))],
        out_specs=pl.BlockSpec((tm, d), lambda i: (i, 0)),
        out_shape=jax.ShapeDtypeStruct((t, d), F32),
        compiler_params=_params(("parallel",)),
        name="final_rmsnorm",
    )(x, g.reshape(1, d))


FFT_S2 = 128


def _dft_tables(seq, ch):
    s1n = seq // FFT_S2
    k1 = np.arange(s1n, dtype=np.float64)
    ang1 = 2.0 * np.pi * np.outer(k1, k1) / s1n
    m1 = np.concatenate([np.cos(ang1), -np.sin(ang1)], axis=0)
    s2 = np.arange(FFT_S2, dtype=np.float64)
    kk = k1[:, None, None] + s1n * s2[None, :, None]
    ang2 = 2.0 * np.pi * ((kk * s2[None, None, :]) % seq) / seq
    gc, gs = np.cos(ang2), np.sin(ang2)
    g = np.concatenate([np.concatenate([gc, gs], axis=2),
                        np.concatenate([-gs, gc], axis=2)], axis=1)
    c = np.arange(ch, dtype=np.float64)
    angc = 2.0 * np.pi * (np.outer(c, c) % ch) / ch
    cs = np.concatenate([np.cos(angc), np.sin(angc)], axis=0) / math.sqrt(seq * ch)
    return (jnp.asarray(m1, F32).astype(BF16), jnp.asarray(g, F32).astype(BF16),
            jnp.asarray(cs, F32).astype(BF16))


def _fft_stage1_kernel(m_ref, u_ref, y_ref):
    for j in range(u_ref.shape[1]):
        y_ref[:, j, :] = jnp.dot(m_ref[...], u_ref[:, j, :], preferred_element_type=F32).astype(y_ref.dtype)


def _fft_stage1(u, m1, *, tn=1024):
    b, s1n, _, d = u.shape
    tn = _pick(d, tn)
    sb = BF16_SUBLANES
    return pl.pallas_call(
        _fft_stage1_kernel,
        grid=(b, FFT_S2 // sb, d // tn),
        in_specs=[pl.BlockSpec((2 * s1n, s1n), lambda i, j, c: (0, 0)),
                  pl.BlockSpec((None, s1n, sb, tn), lambda i, j, c: (i, 0, j, c))],
        out_specs=pl.BlockSpec((None, 2 * s1n, sb, tn), lambda i, j, c: (i, 0, j, c)),
        out_shape=jax.ShapeDtypeStruct((b, 2 * s1n, FFT_S2, d), BF16),
        compiler_params=_params(("parallel", "parallel", "parallel")),
        name="fft_stage1",
    )(m1, u)


def _fft_stage2_kernel(g_ref, yr_ref, yi_ref, cs_ref, f_ref):
    for j in range(g_ref.shape[0]):
        y = jnp.concatenate([yr_ref[j], yi_ref[j]], axis=0)
        xc = jnp.dot(g_ref[j], y, preferred_element_type=F32).astype(BF16)
        xg = jnp.concatenate([xc[:FFT_S2], xc[FFT_S2:]], axis=1)
        f_ref[:, j, :] = jnp.dot(xg, cs_ref[...], preferred_element_type=F32).astype(f_ref.dtype)


def _fft_stage2(y, g, cs):
    b, s1n2, _, d = y.shape
    s1n = s1n2 // 2
    ch = cs.shape[1]
    kb = BF16_SUBLANES
    assert s1n % kb == 0
    nk = s1n // kb
    return pl.pallas_call(
        _fft_stage2_kernel,
        grid=(b, nk, d // ch),
        in_specs=[pl.BlockSpec((kb, 2 * FFT_S2, 2 * FFT_S2), lambda i, j, q: (j, 0, 0)),
                  pl.BlockSpec((None, kb, FFT_S2, ch), lambda i, j, q: (i, j, 0, q)),
                  pl.BlockSpec((None, kb, FFT_S2, ch), lambda i, j, q: (i, j + nk, 0, q)),
                  pl.BlockSpec(cs.shape, lambda i, j, q: (0, 0))],
        out_specs=pl.BlockSpec((None, FFT_S2, kb, ch), lambda i, j, q: (i, 0, j, q)),
        out_shape=jax.ShapeDtypeStruct((b, FFT_S2, s1n, d), BF16),
        compiler_params=_params(("parallel", "parallel", "parallel")),
        name="fft_stage2",
    )(g, y, y, cs)


def _fourier_layer(x, g_norm, w_in, w_out):
    b, s, d = x.shape
    assert s % FFT_S2 == 0 and d % A_GROUPS == 0
    s1n = s // FFT_S2
    m1, g, cs = _dft_tables(s, d // A_GROUPS)
    u = _norm_matmul(x, g_norm, w_in)
    y = _fft_stage1(u.reshape(b, s1n, FFT_S2, d), m1)
    f = _fft_stage2(y, g, cs)
    return _matmul_res(f.reshape(b * s, d), w_out, x.reshape(b * s, d)).reshape(b, s, d)


DIL_TQ = 128
DIL_HALF = 64


def _t5_bucket_np(rel):
    nb = T5_BUCKETS // 2
    max_exact = nb // 2
    n = np.abs(rel)
    large = max_exact + (np.log(np.maximum(n, 1) / max_exact) / math.log(T5_MAX_DIST / max_exact)
                         * (nb - max_exact)).astype(np.int32)
    large = np.minimum(large, nb - 1)
    return (np.where(rel > 0, nb, 0) + np.where(n < max_exact, n, large)).astype(np.int32)


def _dilated_bias(t5_bias_g, dilation):
    off = np.arange(DIL_TQ + 2 * DIL_HALF)[None, :] - DIL_HALF - np.arange(DIL_TQ)[:, None]
    band = np.abs(off) <= DIL_HALF
    bias = jnp.take(t5_bias_g, _t5_bucket_np(off * dilation), axis=0).transpose(2, 0, 1)
    return jnp.where(band[None], bias.astype(F32), NEG_INF)


def _dilated_attn_kernel(bias_ref, q_ref, kp_ref, kc_ref, kn_ref, vp_ref, vc_ref, vn_ref,
                         o_ref, l_ref, *, heads, sub_len):
    i = pl.program_id(2)
    nk = DIL_TQ + 2 * DIL_HALF
    kpos = i * DIL_TQ - DIL_HALF + lax.broadcasted_iota(jnp.int32, (1, nk), 1)
    valid = (kpos >= 0) & (kpos < sub_len)
    lane = lax.broadcasted_iota(jnp.int32, (DIL_TQ, LANES), 1)
    lse_all = jnp.zeros((DIL_TQ, LANES), F32)
    scale = HEAD_DIM ** -0.5
    for h in range(heads):
        hs = slice(h * HEAD_DIM, (h + 1) * HEAD_DIM)
        k_win = jnp.concatenate([kp_ref[DIL_TQ - DIL_HALF:, hs], kc_ref[:, hs], kn_ref[:DIL_HALF, hs]], axis=0)
        v_win = jnp.concatenate([vp_ref[DIL_TQ - DIL_HALF:, hs], vc_ref[:, hs], vn_ref[:DIL_HALF, hs]], axis=0)
        s = lax.dot_general(q_ref[:, hs], k_win, (((1,), (1,)), ((), ())), preferred_element_type=F32)
        s = jnp.where(valid, s * scale + bias_ref[h], NEG_INF)
        m = jnp.max(s, axis=-1, keepdims=True)
        p = jnp.exp(s - m)
        den = jnp.sum(p, axis=-1, keepdims=True)
        o = jnp.dot((p * (1.0 / den)).astype(BF16), v_win, preferred_element_type=F32)
        o_ref[:, hs] = o.astype(o_ref.dtype)
        lse_all = jnp.where(lane == h, m + jnp.log(den), lse_all)
    l_ref[...] = lse_all


def _dilated_attn(qkv, bias, dilation, heads):
    b, r, sub_len, _ = qkv.shape
    hw = heads * HEAD_DIM
    assert r == dilation and sub_len % DIL_TQ == 0
    nb = sub_len // DIL_TQ

    def spec(c, shift):
        return pl.BlockSpec((None, None, DIL_TQ, hw),
                            lambda bi, rho, i: (bi, rho, jnp.clip(i + shift, 0, nb - 1), c))

    return pl.pallas_call(
        functools.partial(_dilated_attn_kernel, heads=heads, sub_len=sub_len),
        grid=(b, r, nb),
        in_specs=[pl.BlockSpec(bias.shape, lambda bi, rho, i: (0, 0, 0)),
                  spec(0, 0), spec(1, -1), spec(1, 0), spec(1, 1), spec(2, -1), spec(2, 0), spec(2, 1)],
        out_specs=[pl.BlockSpec((None, None, DIL_TQ, hw), lambda bi, rho, i: (bi, rho, i, 0)),
                   pl.BlockSpec((None, None, DIL_TQ, LANES), lambda bi, rho, i: (bi, rho, i, 0))],
        out_shape=[jax.ShapeDtypeStruct((b, r, sub_len, hw), BF16),
                   jax.ShapeDtypeStruct((b, r, sub_len, LANES), F32)],
        compiler_params=_params(("parallel", "parallel", "parallel")),
        name=f"dilated_attn_r{dilation}",
    )(bias, *([qkv] * 7))


def _dilated_combine_kernel(o0_ref, o1_ref, o2_ref, l0_ref, l1_ref, l2_ref, w_ref, x_ref, out_ref,
                            a_ref, oi_ref, li_ref, *, heads):
    for g, (o_ref, l_ref) in enumerate(((o0_ref, l0_ref), (o1_ref, l1_ref), (o2_ref, l2_ref))):
        r, per = o_ref.shape[0], o_ref.shape[1]
        for rho in range(r):
            rows = pl.ds(rho, per, stride=r) if r > 1 else slice(None)
            for h in range(heads):
                oi_ref[g, h, rows, :] = o_ref[rho, :, h * HEAD_DIM:(h + 1) * HEAD_DIM].astype(F32)
            li_ref[g, rows, :] = l_ref[rho]
    l0, l1, l2 = li_ref[0], li_ref[1], li_ref[2]
    m = jnp.maximum(jnp.maximum(l0, l1), l2)
    e0, e1, e2 = jnp.exp(l0 - m), jnp.exp(l1 - m), jnp.exp(l2 - m)
    inv = 1.0 / (e0 + e1 + e2)
    a0, a1, a2 = e0 * inv, e1 * inv, e2 * inv
    for h in range(heads):
        hs = slice(h * HEAD_DIM, (h + 1) * HEAD_DIM)
        o = a0[:, h:h + 1] * oi_ref[0, h] + a1[:, h:h + 1] * oi_ref[1, h] + a2[:, h:h + 1] * oi_ref[2, h]
        a_ref[:, hs] = o.astype(BF16)
    out_ref[...] = x_ref[...] + jnp.dot(a_ref[...], w_ref[...], preferred_element_type=F32)


def _dilated_combine(os, ls, w_out, x, *, tm=512):
    b, s, d = x.shape
    hw = w_out.shape[0]
    tm = _pick(s, tm)
    per_b = s // tm

    def dspec(a):
        r, w = a.shape[1], a.shape[3]
        assert tm % (r * BF16_SUBLANES) == 0
        return pl.BlockSpec((None, r, tm // r, w), lambda i: (i // per_b, 0, i % per_b, 0))

    xspec = pl.BlockSpec((None, tm, d), lambda i: (i // per_b, i % per_b, 0))
    return pl.pallas_call(
        functools.partial(_dilated_combine_kernel, heads=hw // HEAD_DIM),
        grid=(b * per_b,),
        in_specs=[dspec(a) for a in os] + [dspec(a) for a in ls] + [pl.BlockSpec((hw, d), lambda i: (0, 0)), xspec],
        out_specs=xspec,
        out_shape=jax.ShapeDtypeStruct((b, s, d), F32),
        scratch_shapes=[pltpu.VMEM((tm, hw), BF16), pltpu.VMEM((3, hw // HEAD_DIM, tm, HEAD_DIM), F32), pltpu.VMEM((3, tm, LANES), F32)],
        compiler_params=_params(("parallel",)),
        name="dilated_combine",
    )(*os, *ls, w_out, x)


def _dilated_layer(x, g_norm, w_qkv, w_out, t5_bias):
    hw = w_out.shape[0]
    heads = hw // HEAD_DIM
    assert len(B_PAIRS) == 3 and w_qkv.shape[1] == 9 * hw
    os, ls = [], []
    for g, (window, dilation) in enumerate(B_PAIRS):
        assert window // (2 * dilation) == DIL_HALF
        qkv = _norm_matmul(x, g_norm, w_qkv, col0=g * 3 * hw, n=3 * hw, dilation=dilation)
        bias = _dilated_bias(t5_bias[:, g * heads:(g + 1) * heads], dilation)
        o, l = _dilated_attn(qkv, bias, dilation, heads)
        os.append(o)
        ls.append(l)
    return _dilated_combine(os, ls, w_out, x)


NA_QROWS = 8
NA_KBLK = 4
NA_WIN = NA_QROWS + NA_ROWS
NA_HEAD_GROUP = 4


def _na_bias_table(rpb):
    scale = HEAD_DIM ** -0.5
    qc = np.arange(GRID_W)[None, :]
    kc = np.arange(GRID_W)[:, None]
    cstart = np.clip(qc - NA_COLS // 2, 0, GRID_W - NA_COLS)
    ok = (kc >= cstart) & (kc < cstart + NA_COLS)
    dcol = np.clip(kc - qc, -(NA_COLS - 1), NA_COLS - 1) + NA_COLS - 1
    tab = jnp.take(rpb.astype(F32), dcol, axis=2)
    tab = jnp.where(ok[None, None], tab, NEG_INF) / scale
    hi = tab.astype(BF16)
    lo = (tab - hi.astype(F32)).astype(BF16)
    out = jnp.concatenate([hi, lo], axis=-1)
    return out.reshape(rpb.shape[0], (2 * NA_ROWS - 1) * GRID_W, 2 * GRID_W)


def _na_attn_kernel(bk_ref, q_ref, k0_ref, k1_ref, k2_ref, k3_ref, v0_ref, v1_ref, v2_ref, v3_ref,
                    o_ref, kw_ref, vw_ref, *, heads, rows):
    rb = pl.program_id(1)
    blk = NA_KBLK * GRID_W
    for j, (kr, vr) in enumerate(((k0_ref, v0_ref), (k1_ref, v1_ref), (k2_ref, v2_ref), (k3_ref, v3_ref))):
        kw_ref[j * blk:(j + 1) * blk, :] = kr[...]
        vw_ref[j * blk:(j + 1) * blk, :] = vr[...]
    ws = NA_KBLK * jnp.clip(2 * rb - 1, 0, rows // NA_KBLK - NA_WIN // NA_KBLK)
    nkeys = NA_ROWS * GRID_W
    eye = (lax.broadcasted_iota(jnp.int32, (GRID_W, 2 * GRID_W), 0)
           == lax.broadcasted_iota(jnp.int32, (GRID_W, 2 * GRID_W), 1) % GRID_W)
    eye2 = jnp.where(eye, 1.0, 0.0).astype(BF16)
    scale = HEAD_DIM ** -0.5
    nt = (((1,), (1,)), ((), ()))

    def row_body(rr, carry):
        r = rb * NA_QROWS + rr
        rs = jnp.clip(r - NA_ROWS // 2, 0, rows - NA_ROWS)
        ksl = pl.ds(pl.multiple_of((rs - ws) * GRID_W, GRID_W), nkeys)
        bsl = pl.ds(pl.multiple_of((rs - r + NA_ROWS - 1) * GRID_W, GRID_W), nkeys)
        qsl = pl.ds(pl.multiple_of(rr * GRID_W, GRID_W), GRID_W)
        for h0 in range(0, heads, NA_HEAD_GROUP):
            hss = [slice(h * HEAD_DIM, (h + 1) * HEAD_DIM) for h in range(h0, h0 + NA_HEAD_GROUP)]
            ss = [lax.dot_general(q_ref[qsl, hs], kw_ref[ksl, hs], nt, preferred_element_type=F32)
                  + lax.dot_general(eye2, bk_ref[h0 + j, bsl, :], nt, preferred_element_type=F32)
                  for j, hs in enumerate(hss)]
            ps = []
            for s in ss:
                s = s * scale
                p = jnp.exp(s - jnp.max(s, axis=-1, keepdims=True))
                ps.append((p * (1.0 / jnp.sum(p, axis=-1, keepdims=True))).astype(BF16))
            for p, hs in zip(ps, hss):
                o_ref[qsl, hs] = jnp.dot(p, vw_ref[ksl, hs], preferred_element_type=F32).astype(o_ref.dtype)
        return carry

    lax.fori_loop(0, NA_QROWS, row_body, 0)


def _na_attn(qkv, bk, heads):
    b, s, _ = qkv.shape
    hw = heads * HEAD_DIM
    rows = s // GRID_W
    assert s % GRID_W == 0 and rows % NA_QROWS == 0 and rows >= NA_WIN and 2 * GRID_W == LANES
    assert heads % NA_HEAD_GROUP == 0
    nwb = rows // NA_KBLK - NA_WIN // NA_KBLK
    blk = NA_KBLK * GRID_W

    def kv_spec(c, j):
        return pl.BlockSpec((None, blk, hw), lambda bi, rb: (bi, jnp.clip(2 * rb - 1, 0, nwb) + j, c))

    o = pl.pallas_call(
        functools.partial(_na_attn_kernel, heads=heads, rows=rows),
        grid=(b, rows // NA_QROWS),
        in_specs=[pl.BlockSpec(bk.shape, lambda bi, rb: (0, 0, 0)),
                  pl.BlockSpec((None, NA_QROWS * GRID_W, hw), lambda bi, rb: (bi, rb, 0))]
                 + [kv_spec(1, j) for j in range(4)] + [kv_spec(2, j) for j in range(4)],
        out_specs=pl.BlockSpec((None, NA_QROWS * GRID_W, hw), lambda bi, rb: (bi, rb, 0)),
        out_shape=jax.ShapeDtypeStruct((b, s, hw), BF16),
        scratch_shapes=[pltpu.VMEM((NA_WIN * GRID_W, hw), BF16), pltpu.VMEM((NA_WIN * GRID_W, hw), BF16)],
        compiler_params=_params(("parallel", "parallel")),
        name="na_attn",
    )(bk, *([qkv] * 9))
    return o.reshape(b * s, hw)


def _neighbourhood_layer(x, g_norm, w_qkv, w_out, rpb):
    b, s, d = x.shape
    hw = w_out.shape[0]
    assert w_qkv.shape[1] == 3 * hw
    qkv = _norm_matmul(x, g_norm, w_qkv).reshape(b, s, 3 * hw)
    o = _na_attn(qkv, _na_bias_table(rpb), hw // HEAD_DIM)
    return _matmul_res(o, w_out, x.reshape(b * s, d)).reshape(b, s, d)


def _router_kernel(x_ref, g_ref, wt_ref, aff_ref):
    hn = _rmsnorm_rows(x_ref[...], g_ref[...])
    logits = lax.dot_general(wt_ref[...], hn, (((1,), (1,)), ((), ())),
                             precision=lax.Precision.HIGHEST, preferred_element_type=F32)
    m = jnp.max(logits, axis=0, keepdims=True)
    p = jnp.exp(logits - m)
    aff_ref[...] = p / jnp.sum(p, axis=0, keepdims=True)


def _router(x, g, w_router_t, *, tm=512):
    t, d = x.shape
    e = w_router_t.shape[0]
    tm = _pick(t, tm)
    return pl.pallas_call(
        _router_kernel,
        grid=(t // tm,),
        in_specs=[pl.BlockSpec((tm, d), lambda i: (i, 0)),
                  pl.BlockSpec((1, d), lambda i: (0, 0)),
                  pl.BlockSpec((e, d), lambda i: (0, 0))],
        out_specs=pl.BlockSpec((e, tm), lambda i: (0, i)),
        out_shape=jax.ShapeDtypeStruct((e, t), F32),
        compiler_params=_params(("parallel",)),
        name="moe_router",
    )(x, g.reshape(1, d), w_router_t)


def _topk_mask_kernel(aff_ref, sel_ref, *, cap):
    bits = pltpu.bitcast(aff_ref[...], jnp.int32)
    e, t = bits.shape

    def count(mask):
        return jnp.sum(jnp.where(mask, 1, 0), axis=1, keepdims=True)

    def value_step(_, c):
        lo, hi = c
        mid = lo + (hi - lo) // 2
        ok = count(bits >= mid) >= cap
        return jnp.where(ok, mid, lo), jnp.where(ok, hi, mid)

    lo0 = jnp.zeros((e, 1), jnp.int32)
    hi0 = jnp.full((e, 1), 0x7F800000, jnp.int32)
    thr, _ = lax.fori_loop(0, 31, value_step, (lo0, hi0))
    above = bits > thr
    tie = bits == thr
    need = cap - count(above)
    tok = lax.broadcasted_iota(jnp.int32, (e, t), 1)

    def index_step(_, c):
        lo, hi = c
        mid = lo + (hi - lo) // 2
        ok = count(tie & (tok < mid)) >= need
        return jnp.where(ok, lo, mid), jnp.where(ok, mid, hi)

    _, bound = lax.fori_loop(0, max(1, math.ceil(math.log2(t))), index_step,
                             (jnp.zeros((e, 1), jnp.int32), jnp.full((e, 1), t, jnp.int32)))
    sel_ref[...] = jnp.where(above | (tie & (tok < bound)), 1, 0)


def _topk_mask(aff_t, cap):
    e, t = aff_t.shape
    return pl.pallas_call(
        functools.partial(_topk_mask_kernel, cap=cap),
        grid=(1,),
        in_specs=[pl.BlockSpec((e, t), lambda i: (0, 0))],
        out_specs=pl.BlockSpec((e, t), lambda i: (0, 0)),
        out_shape=jax.ShapeDtypeStruct((e, t), jnp.int32),
        compiler_params=_params(("arbitrary",)),
        name="moe_topk_mask",
    )(aff_t)


X_SEM, OUT_SEM, SCATTER_SEM = 0, 1, 2


def _moe_ffn_kernel(idx_ref, gate_ref, gn_ref, wg_ref, wu_ref, wd_ref, x_hbm, yin_hbm, out_hbm,
                    xg_ref, ob_ref, h_ref, acc_ref, sem, *, tm, nt, nf, n_tiles):
    del yin_hbm
    tile = pl.program_id(0) * nt + pl.program_id(1)
    base = tile * tm
    f = pl.program_id(2)

    def start_rows(hbm, buf, first, s, to_vmem):
        def body(r, c):
            tok = idx_ref[first + r]
            src, dst = hbm.at[pl.ds(tok, 1), :], buf.at[pl.ds(r, 1), :]
            (pltpu.make_async_copy(src, dst, sem.at[s]) if to_vmem
             else pltpu.make_async_copy(dst, src, sem.at[s])).start()
            return c
        lax.fori_loop(0, tm, body, 0)

    def wait_rows(hbm, buf, s, to_vmem):
        whole = hbm.at[pl.ds(0, tm), :]
        (pltpu.make_async_copy(whole, buf, sem.at[s]) if to_vmem
         else pltpu.make_async_copy(buf, whole, sem.at[s])).wait()

    @pl.when(f == 0)
    def _():
        @pl.when(tile == 0)
        def _():
            start_rows(x_hbm, xg_ref, base, X_SEM, True)

        @pl.when(tile > 0)
        def _():
            wait_rows(out_hbm, ob_ref, SCATTER_SEM, False)

        start_rows(out_hbm, ob_ref, base, OUT_SEM, True)
        wait_rows(x_hbm, xg_ref, X_SEM, True)
        h_ref[...] = _rmsnorm_rows(xg_ref[...], gn_ref[...]).astype(BF16)
        acc_ref[...] = jnp.zeros_like(acc_ref)

        @pl.when(tile + 1 < n_tiles)
        def _():
            start_rows(x_hbm, xg_ref, base + tm, X_SEM, True)

    h = h_ref[...]
    g = jnp.dot(h, wg_ref[...], preferred_element_type=F32)
    u = jnp.dot(h, wu_ref[...], preferred_element_type=F32)
    hid = (g * (1.0 / (1.0 + jnp.exp(-g))) * u).astype(BF16)
    acc_ref[...] += jnp.dot(hid, wd_ref[...], preferred_element_type=F32)

    @pl.when(f == nf - 1)
    def _():
        wait_rows(out_hbm, ob_ref, OUT_SEM, True)
        ob_ref[...] = ob_ref[...] + acc_ref[...] * gate_ref[...]
        start_rows(out_hbm, ob_ref, base, SCATTER_SEM, False)

        @pl.when(tile == n_tiles - 1)
        def _():
            wait_rows(out_hbm, ob_ref, SCATTER_SEM, False)


def _moe_ffn(x, idx, gate, g_norm, wg, wu, wd, layer, *, tm=1024, tf=512):
    t, d = x.shape
    e, cap = idx.shape
    fdim = wg.shape[3]
    tm = _pick(cap, tm)
    tf = _pick(fdim, tf)
    nt, nf = cap // tm, fdim // tf
    grid_spec = pltpu.PrefetchScalarGridSpec(
        num_scalar_prefetch=1,
        grid=(e, nt, nf),
        in_specs=[pl.BlockSpec((tm, 1), lambda ei, i, f, idx: (ei * nt + i, 0)),
                  pl.BlockSpec((1, d), lambda ei, i, f, idx: (0, 0)),
                  pl.BlockSpec((None, None, d, tf), lambda ei, i, f, idx: (layer, ei, 0, f)),
                  pl.BlockSpec((None, None, d, tf), lambda ei, i, f, idx: (layer, ei, 0, f)),
                  pl.BlockSpec((None, None, tf, d), lambda ei, i, f, idx: (layer, ei, f, 0)),
                  pl.BlockSpec(memory_space=pl.ANY),
                  pl.BlockSpec(memory_space=pl.ANY)],
        out_specs=pl.BlockSpec(memory_space=pl.ANY),
        scratch_shapes=[pltpu.VMEM((tm, d), F32), pltpu.VMEM((tm, d), F32), pltpu.VMEM((tm, d), BF16),
                        pltpu.VMEM((tm, d), F32), pltpu.SemaphoreType.DMA((3,))],
    )
    return pl.pallas_call(
        functools.partial(_moe_ffn_kernel, tm=tm, nt=nt, nf=nf, n_tiles=e * nt),
        grid_spec=grid_spec,
        out_shape=jax.ShapeDtypeStruct((t, d), F32),
        input_output_aliases={7: 0},
        compiler_params=_params(("arbitrary", "arbitrary", "arbitrary")),
        name="moe_ffn",
    )(idx.reshape(e * cap), gate.reshape(e * cap, 1), g_norm.reshape(1, d), wg, wu, wd, x, x)


def _moe_layer(x, g_norm, w_router, wg, wu, wd, layer):
    b, s, d = x.shape
    t = b * s
    e = w_router.shape[1]
    cap = EC_CAPACITY * t // e
    xt = x.reshape(t, d)
    aff_t = _router(xt, g_norm, w_router.T)
    sel = _topk_mask(aff_t, cap)
    idx = jnp.argsort(1 - sel, axis=1, stable=True)[:, :cap].astype(jnp.int32)
    gate = jnp.take_along_axis(aff_t, idx, axis=1)
    return _moe_ffn(xt, idx, gate, g_norm, wg, wu, wd, layer).reshape(b, s, d)


def _trunk(x, w):
    for i in range(w["norm_mix"].shape[0]):
        kind, j = i % N_MIXERS, i // N_MIXERS
        if kind == 0:
            x = _fourier_layer(x, w["norm_mix"][i], w["a_w_in"][j], w["a_w_out"][j])
        elif kind == 1:
            x = _dilated_layer(x, w["norm_mix"][i], w["b_w_qkv"][j], w["b_w_out"][j], w["t5_bias"])
        else:
            x = _neighbourhood_layer(x, w["norm_mix"][i], w["c_w_qkv"][j], w["c_w_out"][j], w["c_rpb"][j])
        x = _moe_layer(x, w["norm_ffn"][i], w["moe_router"][i], w["moe_w_gate"], w["moe_w_up"],
                       w["moe_w_down"], i)
    b, s, d = x.shape
    return _rmsnorm(x.reshape(b * s, d), w["norm_final"]).reshape(b, s, d)


def kernel(x_prompt, x_sample, norm_mix, norm_ffn, norm_final, a_w_in, a_w_out, b_w_qkv, b_w_out, t5_bias,
           c_w_qkv, c_w_out, c_rpb, moe_router, moe_w_gate, moe_w_up, moe_w_down):
    bf = lambda a: a.astype(BF16)
    w = dict(norm_mix=norm_mix, norm_ffn=norm_ffn, norm_final=norm_final, a_w_in=bf(a_w_in), a_w_out=bf(a_w_out),
             b_w_qkv=bf(b_w_qkv), b_w_out=bf(b_w_out), t5_bias=t5_bias, c_w_qkv=bf(c_w_qkv), c_w_out=bf(c_w_out),
             c_rpb=c_rpb, moe_router=moe_router, moe_w_gate=bf(moe_w_gate), moe_w_up=bf(moe_w_up),
             moe_w_down=bf(moe_w_down))
    return _trunk(x_prompt, w), _trunk(x_sample, w)
```

```python
import functools
import math

import numpy as np
import jax
import jax.numpy as jnp
from jax import lax
from jax.experimental import pallas as pl
from jax.experimental.pallas import tpu as pltpu

F32 = jnp.float32
BF16 = jnp.bfloat16

RMS_EPS = 1e-6
NEG_INF = -1e30
HEAD_DIM = 128
GRID_W = 64
A_GROUPS = 4
B_PAIRS = ((128, 1), (512, 4), (2048, 16))
T5_BUCKETS = 32
T5_MAX_DIST = 1024
NA_ROWS = 8
NA_COLS = 16
EC_CAPACITY = 2
N_MIXERS = 3

V7X_VMEM_BYTES = 64 * 1024 * 1024
VMEM_LIMIT = V7X_VMEM_BYTES - 8 * 1024 * 1024
LANES = 128
BF16_SUBLANES = 16


def _params(sem):
    return pltpu.CompilerParams(dimension_semantics=sem, vmem_limit_bytes=VMEM_LIMIT)


def _pick(n, pref):
    t = min(n, pref)
    while n % t:
        t //= 2
    return t


def _rmsnorm_rows(x, g):
    ms = jnp.mean(x * x, axis=-1, keepdims=True)
    return (x * lax.rsqrt(ms + RMS_EPS)) * g


def _norm_matmul_kernel(x_ref, g_ref, w_ref, o_ref, h_ref, r_ref, *, rows, dilation):
    @pl.when(pl.program_id(1) == 0)
    def _():
        def body(c, carry):
            sl = pl.ds(pl.multiple_of(c * rows, rows), rows)
            h_ref[sl, :] = _rmsnorm_rows(x_ref[sl, :], g_ref[...]).astype(BF16)
            return carry
        lax.fori_loop(0, x_ref.shape[0] // rows, body, 0)

    res = jnp.dot(h_ref[...], w_ref[...], preferred_element_type=F32)
    if dilation == 1:
        o_ref[0] = res.astype(o_ref.dtype)
    else:
        chunks = [slice(c * LANES, (c + 1) * LANES) for c in range(r_ref.shape[0])]
        for c, cs in enumerate(chunks):
            r_ref[c] = res[:, cs]
        per = r_ref.shape[1] // dilation
        for rho in range(dilation):
            for c, cs in enumerate(chunks):
                o_ref[rho, :, cs] = r_ref[c, pl.ds(rho, per, stride=dilation), :].astype(o_ref.dtype)


def _norm_matmul(x, g, w, *, col0=0, n=None, dilation=1, tm=1024, tn=1024):
    b, s, d = x.shape
    n = w.shape[1] if n is None else n
    tm = _pick(s, tm)
    tn = _pick(n, tn)
    assert col0 % tn == 0 and tm % (dilation * BF16_SUBLANES) == 0
    rows = _pick(tm, 256)
    per_b = s // tm
    cb0 = col0 // tn
    out = pl.pallas_call(
        functools.partial(_norm_matmul_kernel, rows=rows, dilation=dilation),
        grid=(b * per_b, n // tn),
        in_specs=[pl.BlockSpec((None, tm, d), lambda i, j: (i // per_b, i % per_b, 0)),
                  pl.BlockSpec((1, d), lambda i, j: (0, 0)),
                  pl.BlockSpec((d, tn), lambda i, j: (0, cb0 + j))],
        out_specs=pl.BlockSpec((None, dilation, tm // dilation, tn), lambda i, j: (i // per_b, 0, i % per_b, j)),
        out_shape=jax.ShapeDtypeStruct((b, dilation, s // dilation, n), BF16),
        scratch_shapes=[pltpu.VMEM((tm, d), BF16),
                        pltpu.VMEM((tn // LANES, tm, LANES) if dilation > 1 else (1, 8, LANES), F32)],
        compiler_params=_params(("parallel", "arbitrary")),
        name="norm_matmul",
    )(x, g.reshape(1, d), w)
    return out


def _matmul_res_kernel(a_ref, w_ref, x_ref, o_ref):
    o_ref[...] = x_ref[...] + jnp.dot(a_ref[...], w_ref[...], preferred_element_type=F32)


def _matmul_res(a, w, x, *, tm=512):
    t, k = a.shape
    d = w.shape[1]
    tm = _pick(t, tm)
    return pl.pallas_call(
        _matmul_res_kernel,
        grid=(t // tm,),
        in_specs=[pl.BlockSpec((tm, k), lambda i: (i, 0)),
                  pl.BlockSpec((k, d), lambda i: (0, 0)),
                  pl.BlockSpec((tm, d), lambda i: (i, 0))],
        out_specs=pl.BlockSpec((tm, d), lambda i: (i, 0)),
        out_shape=jax.ShapeDtypeStruct((t, d), F32),
        compiler_params=_params(("parallel",)),
        name="matmul_res",
    )(a, w, x)


def _rmsnorm_kernel(x_ref, g_ref, o_ref):
    o_ref[...] = _rmsnorm_rows(x_ref[...], g_ref[...])


def _rmsnorm(x, g, *, tm=256):
    t, d = x.shape
    tm = _pick(t, tm)
    return pl.pallas_call(
        _rmsnorm_kernel,
        grid=(t // tm,),
        in_specs=[pl.BlockSpec((tm, d), lambda i: (i, 0)),
                  pl.BlockSpec((1, d), lambda i: (0, 0))],
        out_specs=pl.BlockSpec((tm, d), lambda i: (i, 0)),
        out_shape=jax.ShapeDtypeStruct((t, d), F32),
        compiler_params=_params(("parallel",)),
        name="final_rmsnorm",
    )(x, g.reshape(1, d))


FFT_S2 = 128


def _dft_tables(seq, ch):
    s1n = seq // FFT_S2
    k1 = np.arange(s1n, dtype=np.float64)
    ang1 = 2.0 * np.pi * np.outer(k1, k1) / s1n
    m1 = np.concatenate([np.cos(ang1), -np.sin(ang1)], axis=0)
    s2 = np.arange(FFT_S2, dtype=np.float64)
    kk = k1[:, None, None] + s1n * s2[None, :, None]
    ang2 = 2.0 * np.pi * ((kk * s2[None, None, :]) % seq) / seq
    gc, gs = np.cos(ang2), np.sin(ang2)
    g = np.concatenate([np.concatenate([gc, gs], axis=2),
                        np.concatenate([-gs, gc], axis=2)], axis=1)
    c = np.arange(ch, dtype=np.float64)
    angc = 2.0 * np.pi * (np.outer(c, c) % ch) / ch
    cs = np.concatenate([np.cos(angc), np.sin(angc)], axis=0) / math.sqrt(seq * ch)
    return (jnp.asarray(m1, F32).astype(BF16), jnp.asarray(g, F32).astype(BF16),
            jnp.asarray(cs, F32).astype(BF16))


def _fft_stage1_kernel(m_ref, u_ref, y_ref, uf_ref, yf_ref):
    s1n, sb, tn = u_ref.shape
    chunks = [slice(c * LANES, (c + 1) * LANES) for c in range(tn // LANES)]
    for c, cs in enumerate(chunks):
        uf_ref[c] = u_ref[:, :, cs].astype(F32).reshape(s1n * sb, LANES)
    for j in range(sb):
        rows = jnp.concatenate([uf_ref[c, pl.ds(j, s1n, stride=sb), :] for c in range(len(chunks))], axis=1)
        res = jnp.dot(m_ref[...], rows.astype(BF16), preferred_element_type=F32)
        for c, cs in enumerate(chunks):
            yf_ref[c, pl.ds(j, 2 * s1n, stride=sb), :] = res[:, cs]
    for c, cs in enumerate(chunks):
        y_ref[:, :, cs] = yf_ref[c].reshape(2 * s1n, sb, LANES).astype(y_ref.dtype)


def _fft_stage1(u, m1, *, tn=512):
    b, s1n, _, d = u.shape
    tn = _pick(d, tn)
    sb = BF16_SUBLANES
    return pl.pallas_call(
        _fft_stage1_kernel,
        grid=(b, FFT_S2 // sb, d // tn),
        in_specs=[pl.BlockSpec((2 * s1n, s1n), lambda i, j, c: (0, 0)),
                  pl.BlockSpec((None, s1n, sb, tn), lambda i, j, c: (i, 0, j, c))],
        out_specs=pl.BlockSpec((None, 2 * s1n, sb, tn), lambda i, j, c: (i, 0, j, c)),
        out_shape=jax.ShapeDtypeStruct((b, 2 * s1n, FFT_S2, d), BF16),
        scratch_shapes=[pltpu.VMEM((tn // LANES, s1n * sb, LANES), F32),
                        pltpu.VMEM((tn // LANES, 2 * s1n * sb, LANES), F32)],
        compiler_params=_params(("parallel", "parallel", "parallel")),
        name="fft_stage1",
    )(m1, u)


def _fft_stage2_kernel(g_ref, yr_ref, yi_ref, cs_ref, f_ref, ff_ref):
    kb = g_ref.shape[0]
    for j in range(kb):
        y = jnp.concatenate([yr_ref[j], yi_ref[j]], axis=0)
        xc = jnp.dot(g_ref[j], y, preferred_element_type=F32).astype(BF16)
        xg = jnp.concatenate([xc[:FFT_S2], xc[FFT_S2:]], axis=1)
        ff_ref[:, j, :] = jnp.dot(xg, cs_ref[...], preferred_element_type=F32)
    f_ref[...] = ff_ref[...].astype(f_ref.dtype)


def _fft_stage2(y, g, cs):
    b, s1n2, _, d = y.shape
    s1n = s1n2 // 2
    ch = cs.shape[1]
    kb = BF16_SUBLANES
    assert s1n % kb == 0
    nk = s1n // kb
    return pl.pallas_call(
        _fft_stage2_kernel,
        grid=(b, nk, d // ch),
        in_specs=[pl.BlockSpec((kb, 2 * FFT_S2, 2 * FFT_S2), lambda i, j, q: (j, 0, 0)),
                  pl.BlockSpec((None, kb, FFT_S2, ch), lambda i, j, q: (i, j, 0, q)),
                  pl.BlockSpec((None, kb, FFT_S2, ch), lambda i, j, q: (i, j + nk, 0, q)),
                  pl.BlockSpec(cs.shape, lambda i, j, q: (0, 0))],
        out_specs=pl.BlockSpec((None, FFT_S2, kb, ch), lambda i, j, q: (i, 0, j, q)),
        out_shape=jax.ShapeDtypeStruct((b, FFT_S2, s1n, d), BF16),
        scratch_shapes=[pltpu.VMEM((FFT_S2, kb, ch), F32)],
        compiler_params=_params(("parallel", "parallel", "parallel")),
        name="fft_stage2",
    )(g, y, y, cs)


def _fourier_layer(x, g_norm, w_in, w_out):
    b, s, d = x.shape
    assert s % FFT_S2 == 0 and d % A_GROUPS == 0
    s1n = s // FFT_S2
    m1, g, cs = _dft_tables(s, d // A_GROUPS)
    u = _norm_matmul(x, g_norm, w_in)
    y = _fft_stage1(u.reshape(b, s1n, FFT_S2, d), m1)
    f = _fft_stage2(y, g, cs)
    return _matmul_res(f.reshape(b * s, d), w_out, x.reshape(b * s, d)).reshape(b, s, d)


DIL_TQ = 128
DIL_HALF = 64
DIL_HEAD_GROUP = 4


def _t5_bucket_np(rel):
    nb = T5_BUCKETS // 2
    max_exact = nb // 2
    n = np.abs(rel)
    large = max_exact + (np.log(np.maximum(n, 1) / max_exact) / math.log(T5_MAX_DIST / max_exact)
                         * (nb - max_exact)).astype(np.int32)
    large = np.minimum(large, nb - 1)
    return (np.where(rel > 0, nb, 0) + np.where(n < max_exact, n, large)).astype(np.int32)


def _dilated_bias(t5_bias_g, dilation):
    off = np.arange(DIL_TQ + 2 * DIL_HALF)[None, :] - DIL_HALF - np.arange(DIL_TQ)[:, None]
    band = np.abs(off) <= DIL_HALF
    bias = jnp.take(t5_bias_g, _t5_bucket_np(off * dilation), axis=0).transpose(2, 0, 1)
    return jnp.where(band[None], bias.astype(F32), NEG_INF)


def _dilated_attn_kernel(bias_ref, q_ref, kp_ref, kc_ref, kn_ref, vp_ref, vc_ref, vn_ref,
                         o_ref, l_ref, *, heads, sub_len):
    i = pl.program_id(2)
    nk = DIL_TQ + 2 * DIL_HALF
    kpos = i * DIL_TQ - DIL_HALF + lax.broadcasted_iota(jnp.int32, (1, nk), 1)
    valid = (kpos >= 0) & (kpos < sub_len)
    lane = lax.broadcasted_iota(jnp.int32, (DIL_TQ, LANES), 1)
    lse_all = jnp.zeros((DIL_TQ, LANES), F32)
    scale = HEAD_DIM ** -0.5
    def window(p_ref, c_ref, n_ref, hs):
        return jnp.concatenate([p_ref[DIL_TQ - DIL_HALF:, hs], c_ref[:, hs], n_ref[:DIL_HALF, hs]], axis=0)

    for h0 in range(0, heads, DIL_HEAD_GROUP):
        hss = [slice(h * HEAD_DIM, (h + 1) * HEAD_DIM) for h in range(h0, h0 + DIL_HEAD_GROUP)]
        ss = [lax.dot_general(q_ref[:, hs], window(kp_ref, kc_ref, kn_ref, hs), (((1,), (1,)), ((), ())),
                              preferred_element_type=F32) for hs in hss]
        ps = []
        for j, s in enumerate(ss):
            s = jnp.where(valid, s * scale + bias_ref[h0 + j], NEG_INF)
            m = jnp.max(s, axis=-1, keepdims=True)
            p = jnp.exp(s - m)
            den = jnp.sum(p, axis=-1, keepdims=True)
            ps.append((p * (1.0 / den)).astype(BF16))
            lse_all = jnp.where(lane == h0 + j, m + jnp.log(den), lse_all)
        for p, hs in zip(ps, hss):
            o = jnp.dot(p, window(vp_ref, vc_ref, vn_ref, hs), preferred_element_type=F32)
            o_ref[:, hs] = o.astype(o_ref.dtype)
    l_ref[...] = lse_all


def _dilated_attn(qkv, bias, dilation, heads):
    b, r, sub_len, _ = qkv.shape
    hw = heads * HEAD_DIM
    assert r == dilation and sub_len % DIL_TQ == 0
    nb = sub_len // DIL_TQ

    def spec(c, shift):
        return pl.BlockSpec((None, None, DIL_TQ, hw),
                            lambda bi, rho, i: (bi, rho, jnp.clip(i + shift, 0, nb - 1), c))

    return pl.pallas_call(
        functools.partial(_dilated_attn_kernel, heads=heads, sub_len=sub_len),
        grid=(b, r, nb),
        in_specs=[pl.BlockSpec(bias.shape, lambda bi, rho, i: (0, 0, 0)),
                  spec(0, 0), spec(1, -1), spec(1, 0), spec(1, 1), spec(2, -1), spec(2, 0), spec(2, 1)],
        out_specs=[pl.BlockSpec((None, None, DIL_TQ, hw), lambda bi, rho, i: (bi, rho, i, 0)),
                   pl.BlockSpec((None, None, DIL_TQ, LANES), lambda bi, rho, i: (bi, rho, i, 0))],
        out_shape=[jax.ShapeDtypeStruct((b, r, sub_len, hw), BF16),
                   jax.ShapeDtypeStruct((b, r, sub_len, LANES), F32)],
        compiler_params=_params(("parallel", "parallel", "parallel")),
        name=f"dilated_attn_r{dilation}",
    )(bias, *([qkv] * 7))


def _dilated_combine_kernel(o0_ref, o1_ref, o2_ref, l0_ref, l1_ref, l2_ref, w_ref, x_ref, out_ref,
                            a_ref, oi_ref, li_ref, *, heads):
    for g, (o_ref, l_ref) in enumerate(((o0_ref, l0_ref), (o1_ref, l1_ref), (o2_ref, l2_ref))):
        r, per = o_ref.shape[0], o_ref.shape[1]
        for rho in range(r):
            rows = pl.ds(rho, per, stride=r) if r > 1 else slice(None)
            for h in range(heads):
                oi_ref[g, h, rows, :] = o_ref[rho, :, h * HEAD_DIM:(h + 1) * HEAD_DIM].astype(F32)
            li_ref[g, rows, :] = l_ref[rho]
    l0, l1, l2 = li_ref[0], li_ref[1], li_ref[2]
    m = jnp.maximum(jnp.maximum(l0, l1), l2)
    e0, e1, e2 = jnp.exp(l0 - m), jnp.exp(l1 - m), jnp.exp(l2 - m)
    inv = 1.0 / (e0 + e1 + e2)
    a0, a1, a2 = e0 * inv, e1 * inv, e2 * inv
    for h in range(heads):
        hs = slice(h * HEAD_DIM, (h + 1) * HEAD_DIM)
        o = a0[:, h:h + 1] * oi_ref[0, h] + a1[:, h:h + 1] * oi_ref[1, h] + a2[:, h:h + 1] * oi_ref[2, h]
        a_ref[:, hs] = o.astype(BF16)
    out_ref[...] = x_ref[...] + jnp.dot(a_ref[...], w_ref[...], preferred_element_type=F32)


def _dilated_combine(os, ls, w_out, x, *, tm=512):
    b, s, d = x.shape
    hw = w_out.shape[0]
    tm = _pick(s, tm)
    per_b = s // tm

    def dspec(a):
        r, w = a.shape[1], a.shape[3]
        assert tm % (r * BF16_SUBLANES) == 0
        return pl.BlockSpec((None, r, tm // r, w), lambda i: (i // per_b, 0, i % per_b, 0))

    xspec = pl.BlockSpec((None, tm, d), lambda i: (i // per_b, i % per_b, 0))
    return pl.pallas_call(
        functools.partial(_dilated_combine_kernel, heads=hw // HEAD_DIM),
        grid=(b * per_b,),
        in_specs=[dspec(a) for a in os] + [dspec(a) for a in ls] + [pl.BlockSpec((hw, d), lambda i: (0, 0)), xspec],
        out_specs=xspec,
        out_shape=jax.ShapeDtypeStruct((b, s, d), F32),
        scratch_shapes=[pltpu.VMEM((tm, hw), BF16), pltpu.VMEM((3, hw // HEAD_DIM, tm, HEAD_DIM), F32), pltpu.VMEM((3, tm, LANES), F32)],
        compiler_params=_params(("parallel",)),
        name="dilated_combine",
    )(*os, *ls, w_out, x)


def _dilated_layer(x, g_norm, w_qkv, w_out, t5_bias):
    hw = w_out.shape[0]
    heads = hw // HEAD_DIM
    assert len(B_PAIRS) == 3 and w_qkv.shape[1] == 9 * hw
    os, ls = [], []
    for g, (window, dilation) in enumerate(B_PAIRS):
        assert window // (2 * dilation) == DIL_HALF
        qkv = _norm_matmul(x, g_norm, w_qkv, col0=g * 3 * hw, n=3 * hw, dilation=dilation)
        bias = _dilated_bias(t5_bias[:, g * heads:(g + 1) * heads], dilation)
        o, l = _dilated_attn(qkv, bias, dilation, heads)
        os.append(o)
        ls.append(l)
    return _dilated_combine(os, ls, w_out, x)


NA_QROWS = 8
NA_KBLK = 4
NA_WIN = NA_QROWS + NA_ROWS
NA_HEAD_GROUP = 4


def _na_bias_table(rpb):
    scale = HEAD_DIM ** -0.5
    qc = np.arange(GRID_W)[None, :]
    kc = np.arange(GRID_W)[:, None]
    cstart = np.clip(qc - NA_COLS // 2, 0, GRID_W - NA_COLS)
    ok = (kc >= cstart) & (kc < cstart + NA_COLS)
    dcol = np.clip(kc - qc, -(NA_COLS - 1), NA_COLS - 1) + NA_COLS - 1
    tab = jnp.take(rpb.astype(F32), dcol, axis=2)
    tab = jnp.where(ok[None, None], tab, NEG_INF) / scale
    hi = tab.astype(BF16)
    lo = (tab - hi.astype(F32)).astype(BF16)
    out = jnp.concatenate([hi, lo], axis=-1)
    return out.reshape(rpb.shape[0], (2 * NA_ROWS - 1) * GRID_W, 2 * GRID_W)


def _na_attn_kernel(bk_ref, q_ref, k0_ref, k1_ref, k2_ref, k3_ref, v0_ref, v1_ref, v2_ref, v3_ref,
                    o_ref, kw_ref, vw_ref, *, heads, rows):
    rb = pl.program_id(1)
    blk = NA_KBLK * GRID_W
    for j, (kr, vr) in enumerate(((k0_ref, v0_ref), (k1_ref, v1_ref), (k2_ref, v2_ref), (k3_ref, v3_ref))):
        kw_ref[j * blk:(j + 1) * blk, :] = kr[...]
        vw_ref[j * blk:(j + 1) * blk, :] = vr[...]
    ws = NA_KBLK * jnp.clip(2 * rb - 1, 0, rows // NA_KBLK - NA_WIN // NA_KBLK)
    nkeys = NA_ROWS * GRID_W
    eye = (lax.broadcasted_iota(jnp.int32, (GRID_W, 2 * GRID_W), 0)
           == lax.broadcasted_iota(jnp.int32, (GRID_W, 2 * GRID_W), 1) % GRID_W)
    eye2 = jnp.where(eye, 1.0, 0.0).astype(BF16)
    scale = HEAD_DIM ** -0.5
    nt = (((1,), (1,)), ((), ()))

    def row_body(rr, carry):
        r = rb * NA_QROWS + rr
        rs = jnp.clip(r - NA_ROWS // 2, 0, rows - NA_ROWS)
        ksl = pl.ds(pl.multiple_of((rs - ws) * GRID_W, GRID_W), nkeys)
        bsl = pl.ds(pl.multiple_of((rs - r + NA_ROWS - 1) * GRID_W, GRID_W), nkeys)
        qsl = pl.ds(pl.multiple_of(rr * GRID_W, GRID_W), GRID_W)
        for h0 in range(0, heads, NA_HEAD_GROUP):
            hss = [slice(h * HEAD_DIM, (h + 1) * HEAD_DIM) for h in range(h0, h0 + NA_HEAD_GROUP)]
            ss = [lax.dot_general(q_ref[qsl, hs], kw_ref[ksl, hs], nt, preferred_element_type=F32)
                  + lax.dot_general(eye2, bk_ref[h0 + j, bsl, :], nt, preferred_element_type=F32)
                  for j, hs in enumerate(hss)]
            ps = []
            for s in ss:
                s = s * scale
                p = jnp.exp(s - jnp.max(s, axis=-1, keepdims=True))
                ps.append((p * (1.0 / jnp.sum(p, axis=-1, keepdims=True))).astype(BF16))
            for p, hs in zip(ps, hss):
                o_ref[qsl, hs] = jnp.dot(p, vw_ref[ksl, hs], preferred_element_type=F32).astype(o_ref.dtype)
        return carry

    lax.fori_loop(0, NA_QROWS, row_body, 0)


def _na_attn(qkv, bk, heads):
    b, s, _ = qkv.shape
    hw = heads * HEAD_DIM
    rows = s // GRID_W
    assert s % GRID_W == 0 and rows % NA_QROWS == 0 and rows >= NA_WIN and 2 * GRID_W == LANES
    assert heads % NA_HEAD_GROUP == 0
    nwb = rows // NA_KBLK - NA_WIN // NA_KBLK
    blk = NA_KBLK * GRID_W

    def kv_spec(c, j):
        return pl.BlockSpec((None, blk, hw), lambda bi, rb: (bi, jnp.clip(2 * rb - 1, 0, nwb) + j, c))

    o = pl.pallas_call(
        functools.partial(_na_attn_kernel, heads=heads, rows=rows),
        grid=(b, rows // NA_QROWS),
        in_specs=[pl.BlockSpec(bk.shape, lambda bi, rb: (0, 0, 0)),
                  pl.BlockSpec((None, NA_QROWS * GRID_W, hw), lambda bi, rb: (bi, rb, 0))]
                 + [kv_spec(1, j) for j in range(4)] + [kv_spec(2, j) for j in range(4)],
        out_specs=pl.BlockSpec((None, NA_QROWS * GRID_W, hw), lambda bi, rb: (bi, rb, 0)),
        out_shape=jax.ShapeDtypeStruct((b, s, hw), BF16),
        scratch_shapes=[pltpu.VMEM((NA_WIN * GRID_W, hw), BF16), pltpu.VMEM((NA_WIN * GRID_W, hw), BF16)],
        compiler_params=_params(("parallel", "parallel")),
        name="na_attn",
    )(bk, *([qkv] * 9))
    return o.reshape(b * s, hw)


def _neighbourhood_layer(x, g_norm, w_qkv, w_out, rpb):
    b, s, d = x.shape
    hw = w_out.shape[0]
    assert w_qkv.shape[1] == 3 * hw
    qkv = _norm_matmul(x, g_norm, w_qkv).reshape(b, s, 3 * hw)
    o = _na_attn(qkv, _na_bias_table(rpb), hw // HEAD_DIM)
    return _matmul_res(o, w_out, x.reshape(b * s, d)).reshape(b, s, d)


def _router_kernel(x_ref, g_ref, wt_ref, aff_ref):
    hn = _rmsnorm_rows(x_ref[...], g_ref[...]).astype(BF16)
    logits = lax.dot_general(wt_ref[...], hn, (((1,), (1,)), ((), ())), preferred_element_type=F32)
    m = jnp.max(logits, axis=0, keepdims=True)
    p = jnp.exp(logits - m)
    aff_ref[...] = p / jnp.sum(p, axis=0, keepdims=True)


def _router(x, g, w_router_t, *, tm=512):
    t, d = x.shape
    e = w_router_t.shape[0]
    tm = _pick(t, tm)
    return pl.pallas_call(
        _router_kernel,
        grid=(t // tm,),
        in_specs=[pl.BlockSpec((tm, d), lambda i: (i, 0)),
                  pl.BlockSpec((1, d), lambda i: (0, 0)),
                  pl.BlockSpec((e, d), lambda i: (0, 0))],
        out_specs=pl.BlockSpec((e, tm), lambda i: (0, i)),
        out_shape=jax.ShapeDtypeStruct((e, t), F32),
        compiler_params=_params(("parallel",)),
        name="moe_router",
    )(x, g.reshape(1, d), w_router_t)


def _topk_mask_kernel(aff_ref, sel_ref, *, cap):
    bits = pltpu.bitcast(aff_ref[...], jnp.int32)
    e, t = bits.shape

    def count(mask):
        return jnp.sum(jnp.where(mask, 1, 0), axis=1, keepdims=True)

    def value_step(_, c):
        lo, hi = c
        mid = lo + (hi - lo) // 2
        ok = count(bits >= mid) >= cap
        return jnp.where(ok, mid, lo), jnp.where(ok, hi, mid)

    lo0 = jnp.zeros((e, 1), jnp.int32)
    hi0 = jnp.full((e, 1), 0x7F800000, jnp.int32)
    thr, _ = lax.fori_loop(0, 31, value_step, (lo0, hi0))
    above = bits > thr
    tie = bits == thr
    need = cap - count(above)
    tok = lax.broadcasted_iota(jnp.int32, (e, t), 1)

    def index_step(_, c):
        lo, hi = c
        mid = lo + (hi - lo) // 2
        ok = count(tie & (tok < mid)) >= need
        return jnp.where(ok, lo, mid), jnp.where(ok, mid, hi)

    _, bound = lax.fori_loop(0, max(1, math.ceil(math.log2(t))), index_step,
                             (jnp.zeros((e, 1), jnp.int32), jnp.full((e, 1), t, jnp.int32)))
    sel_ref[...] = jnp.where(above | (tie & (tok < bound)), 1, 0)


def _topk_mask(aff_t, cap):
    e, t = aff_t.shape
    return pl.pallas_call(
        functools.partial(_topk_mask_kernel, cap=cap),
        grid=(1,),
        in_specs=[pl.BlockSpec((e, t), lambda i: (0, 0))],
        out_specs=pl.BlockSpec((e, t), lambda i: (0, 0)),
        out_shape=jax.ShapeDtypeStruct((e, t), jnp.int32),
        compiler_params=_params(("arbitrary",)),
        name="moe_topk_mask",
    )(aff_t)


X_SEM, OUT_SEM, SCATTER_SEM = 0, 1, 2


def _moe_ffn_kernel(idx_ref, gate_ref, gn_ref, wg_ref, wu_ref, wd_ref, x_hbm, yin_hbm, out_hbm,
                    xg_ref, ob_ref, h_ref, acc_ref, sem, *, tm, nt, nf, n_tiles):
    del yin_hbm
    tile = pl.program_id(0) * nt + pl.program_id(1)
    base = tile * tm
    f = pl.program_id(2)

    def row_copy(hbm, buf, first, r, s, to_vmem):
        tok = idx_ref[first + r]
        src, dst = hbm.at[pl.ds(tok, 1), :], buf.at[pl.ds(r, 1), :]
        return (pltpu.make_async_copy(src, dst, sem.at[s]) if to_vmem
                else pltpu.make_async_copy(dst, src, sem.at[s]))

    def start_rows(hbm, buf, first, s, to_vmem):
        def body(r, c):
            row_copy(hbm, buf, first, r, s, to_vmem).start()
            return c
        lax.fori_loop(0, tm, body, 0, unroll=8)

    def wait_rows(hbm, buf, s, to_vmem):
        whole = hbm.at[pl.ds(0, tm), :]
        (pltpu.make_async_copy(whole, buf, sem.at[s]) if to_vmem
         else pltpu.make_async_copy(buf, whole, sem.at[s])).wait()

    @pl.when(f == 0)
    def _():
        @pl.when(tile == 0)
        def _():
            start_rows(x_hbm, xg_ref, base, X_SEM, True)

        @pl.when(tile > 0)
        def _():
            wait_rows(out_hbm, ob_ref, SCATTER_SEM, False)

        wait_rows(x_hbm, xg_ref, X_SEM, True)
        h_ref[...] = _rmsnorm_rows(xg_ref[...], gn_ref[...]).astype(BF16)
        acc_ref[...] = jnp.zeros_like(acc_ref)

    nxt = jnp.minimum(tile + 1, n_tiles - 1) * tm
    per_step = tm // nf
    for r in range(per_step):
        row = f * per_step + r
        row_copy(out_hbm, ob_ref, base, row, OUT_SEM, True).start()
        row_copy(x_hbm, xg_ref, nxt, row, X_SEM, True).start()

    h = h_ref[...]
    g = jnp.dot(h, wg_ref[...], preferred_element_type=F32)
    u = jnp.dot(h, wu_ref[...], preferred_element_type=F32)
    hid = (g * (1.0 / (1.0 + jnp.exp(-g))) * u).astype(BF16)
    acc_ref[...] += jnp.dot(hid, wd_ref[...], preferred_element_type=F32)

    @pl.when(f == nf - 1)
    def _():
        wait_rows(out_hbm, ob_ref, OUT_SEM, True)
        ob_ref[...] = ob_ref[...] + acc_ref[...] * gate_ref[...]
        start_rows(out_hbm, ob_ref, base, SCATTER_SEM, False)

        @pl.when(tile == n_tiles - 1)
        def _():
            wait_rows(out_hbm, ob_ref, SCATTER_SEM, False)
            wait_rows(x_hbm, xg_ref, X_SEM, True)


def _moe_ffn(x, idx, gate, g_norm, wg, wu, wd, layer, *, tm=1024, tf=512):
    t, d = x.shape
    e, cap = idx.shape
    fdim = wg.shape[3]
    tm = _pick(cap, tm)
    tf = _pick(fdim, tf)
    nt, nf = cap // tm, fdim // tf
    assert tm % nf == 0
    grid_spec = pltpu.PrefetchScalarGridSpec(
        num_scalar_prefetch=1,
        grid=(e, nt, nf),
        in_specs=[pl.BlockSpec((tm, 1), lambda ei, i, f, idx: (ei * nt + i, 0)),
                  pl.BlockSpec((1, d), lambda ei, i, f, idx: (0, 0)),
                  pl.BlockSpec((None, None, d, tf), lambda ei, i, f, idx: (layer, ei, 0, f)),
                  pl.BlockSpec((None, None, d, tf), lambda ei, i, f, idx: (layer, ei, 0, f)),
                  pl.BlockSpec((None, None, tf, d), lambda ei, i, f, idx: (layer, ei, f, 0)),
                  pl.BlockSpec(memory_space=pl.ANY),
                  pl.BlockSpec(memory_space=pl.ANY)],
        out_specs=pl.BlockSpec(memory_space=pl.ANY),
        scratch_shapes=[pltpu.VMEM((tm, d), F32), pltpu.VMEM((tm, d), F32), pltpu.VMEM((tm, d), BF16),
                        pltpu.VMEM((tm, d), F32), pltpu.SemaphoreType.DMA((3,))],
    )
    return pl.pallas_call(
        functools.partial(_moe_ffn_kernel, tm=tm, nt=nt, nf=nf, n_tiles=e * nt),
        grid_spec=grid_spec,
        out_shape=jax.ShapeDtypeStruct((t, d), F32),
        input_output_aliases={7: 0},
        compiler_params=_params(("arbitrary", "arbitrary", "arbitrary")),
        name="moe_ffn",
    )(idx.reshape(e * cap), gate.reshape(e * cap, 1), g_norm.reshape(1, d), wg, wu, wd, x, x)


def _moe_layer(x, g_norm, w_router, wg, wu, wd, layer):
    b, s, d = x.shape
    t = b * s
    e = w_router.shape[1]
    cap = EC_CAPACITY * t // e
    xt = x.reshape(t, d)
    aff_t = _router(xt, g_norm, w_router.T.astype(BF16))
    sel = _topk_mask(aff_t, cap)
    idx = jnp.argsort(1 - sel, axis=1, stable=True)[:, :cap].astype(jnp.int32)
    gate = jnp.take_along_axis(aff_t, idx, axis=1)
    return _moe_ffn(xt, idx, gate, g_norm, wg, wu, wd, layer).reshape(b, s, d)


def _trunk(x, w):
    for i in range(w["norm_mix"].shape[0]):
        kind, j = i % N_MIXERS, i // N_MIXERS
        if kind == 0:
            x = _fourier_layer(x, w["norm_mix"][i], w["a_w_in"][j], w["a_w_out"][j])
        elif kind == 1:
            x = _dilated_layer(x, w["norm_mix"][i], w["b_w_qkv"][j], w["b_w_out"][j], w["t5_bias"])
        else:
            x = _neighbourhood_layer(x, w["norm_mix"][i], w["c_w_qkv"][j], w["c_w_out"][j], w["c_rpb"][j])
        x = _moe_layer(x, w["norm_ffn"][i], w["moe_router"][i], w["moe_w_gate"], w["moe_w_up"],
                       w["moe_w_down"], i)
    b, s, d = x.shape
    return _rmsnorm(x.reshape(b * s, d), w["norm_final"]).reshape(b, s, d)


def kernel(x_prompt, x_sample, norm_mix, norm_ffn, norm_final, a_w_in, a_w_out, b_w_qkv, b_w_out, t5_bias,
           c_w_qkv, c_w_out, c_rpb, moe_router, moe_w_gate, moe_w_up, moe_w_down):
    bf = lambda a: a.astype(BF16)
    w = dict(norm_mix=norm_mix, norm_ffn=norm_ffn, norm_final=norm_final, a_w_in=bf(a_w_in), a_w_out=bf(a_w_out),
             b_w_qkv=bf(b_w_qkv), b_w_out=bf(b_w_out), t5_bias=t5_bias, c_w_qkv=bf(c_w_qkv), c_w_out=bf(c_w_out),
             c_rpb=c_rpb, moe_router=moe_router, moe_w_gate=bf(moe_w_gate), moe_w_up=bf(moe_w_up),
             moe_w_down=bf(moe_w_down))
    return _trunk(x_prompt, w), _trunk(x_sample, w)
```

```python
import functools
import math

import numpy as np
import jax
import jax.numpy as jnp
from jax import lax
from jax.experimental import pallas as pl
from jax.experimental.pallas import tpu as pltpu

F32 = jnp.float32
BF16 = jnp.bfloat16

RMS_EPS = 1e-6
NEG_INF = -1e30
HEAD_DIM = 128
GRID_W = 64
A_GROUPS = 4
B_PAIRS = ((128, 1), (512, 4), (2048, 16))
T5_BUCKETS = 32
T5_MAX_DIST = 1024
NA_ROWS = 8
NA_COLS = 16
EC_CAPACITY = 2
N_MIXERS = 3

V7X_VMEM_BYTES = 64 * 1024 * 1024
VMEM_LIMIT = V7X_VMEM_BYTES - 8 * 1024 * 1024
LANES = 128
BF16_SUBLANES = 16


def _params(sem):
    return pltpu.CompilerParams(dimension_semantics=sem, vmem_limit_bytes=VMEM_LIMIT)


def _pick(n, pref):
    t = min(n, pref)
    while n % t:
        t //= 2
    return t


def _rmsnorm_rows(x, g):
    ms = jnp.mean(x * x, axis=-1, keepdims=True)
    return (x * lax.rsqrt(ms + RMS_EPS)) * g


def _norm_matmul_kernel(x_ref, g_ref, w_ref, o_ref, h_ref, r_ref, *, rows, dilation):
    @pl.when(pl.program_id(1) == 0)
    def _():
        def body(c, carry):
            sl = pl.ds(pl.multiple_of(c * rows, rows), rows)
            h_ref[sl, :] = _rmsnorm_rows(x_ref[sl, :], g_ref[...]).astype(BF16)
            return carry
        lax.fori_loop(0, x_ref.shape[0] // rows, body, 0)

    res = jnp.dot(h_ref[...], w_ref[...], preferred_element_type=F32)
    if dilation == 1:
        o_ref[0] = res.astype(o_ref.dtype)
    else:
        chunks = [slice(c * LANES, (c + 1) * LANES) for c in range(r_ref.shape[0])]
        for c, cs in enumerate(chunks):
            r_ref[c] = res[:, cs]
        per = r_ref.shape[1] // dilation
        for rho in range(dilation):
            for c, cs in enumerate(chunks):
                o_ref[rho, :, cs] = r_ref[c, pl.ds(rho, per, stride=dilation), :].astype(o_ref.dtype)


def _norm_matmul(x, g, w, *, col0=0, n=None, dilation=1, tm=1024, tn=1024):
    b, s, d = x.shape
    n = w.shape[1] if n is None else n
    tm = _pick(s, tm)
    tn = _pick(n, tn)
    assert col0 % tn == 0 and tm % (dilation * BF16_SUBLANES) == 0
    rows = _pick(tm, 256)
    per_b = s // tm
    cb0 = col0 // tn
    out = pl.pallas_call(
        functools.partial(_norm_matmul_kernel, rows=rows, dilation=dilation),
        grid=(b * per_b, n // tn),
        in_specs=[pl.BlockSpec((None, tm, d), lambda i, j: (i // per_b, i % per_b, 0)),
                  pl.BlockSpec((1, d), lambda i, j: (0, 0)),
                  pl.BlockSpec((d, tn), lambda i, j: (0, cb0 + j))],
        out_specs=pl.BlockSpec((None, dilation, tm // dilation, tn), lambda i, j: (i // per_b, 0, i % per_b, j)),
        out_shape=jax.ShapeDtypeStruct((b, dilation, s // dilation, n), BF16),
        scratch_shapes=[pltpu.VMEM((tm, d), BF16),
                        pltpu.VMEM((tn // LANES, tm, LANES) if dilation > 1 else (1, 8, LANES), F32)],
        compiler_params=_params(("parallel", "arbitrary")),
        name="norm_matmul",
    )(x, g.reshape(1, d), w)
    return out


def _matmul_res_kernel(a_ref, w_ref, x_ref, o_ref, o2_ref):
    res = x_ref[...] + jnp.dot(a_ref[...], w_ref[...], preferred_element_type=F32)
    o_ref[...] = res
    o2_ref[...] = res


def _matmul_res(a, w, x, *, tm=512):
    t, k = a.shape
    d = w.shape[1]
    tm = _pick(t, tm)
    row = pl.BlockSpec((tm, d), lambda i: (i, 0))
    return pl.pallas_call(
        _matmul_res_kernel,
        grid=(t // tm,),
        in_specs=[pl.BlockSpec((tm, k), lambda i: (i, 0)), pl.BlockSpec((k, d), lambda i: (0, 0)), row],
        out_specs=[row, row],
        out_shape=[jax.ShapeDtypeStruct((t, d), F32)] * 2,
        compiler_params=_params(("parallel",)),
        name="matmul_res",
    )(a, w, x)


def _rmsnorm_kernel(x_ref, g_ref, o_ref):
    o_ref[...] = _rmsnorm_rows(x_ref[...], g_ref[...])


def _rmsnorm(x, g, *, tm=256):
    t, d = x.shape
    tm = _pick(t, tm)
    return pl.pallas_call(
        _rmsnorm_kernel,
        grid=(t // tm,),
        in_specs=[pl.BlockSpec((tm, d), lambda i: (i, 0)),
                  pl.BlockSpec((1, d), lambda i: (0, 0))],
        out_specs=pl.BlockSpec((tm, d), lambda i: (i, 0)),
        out_shape=jax.ShapeDtypeStruct((t, d), F32),
        compiler_params=_params(("parallel",)),
        name="final_rmsnorm",
    )(x, g.reshape(1, d))


FFT_S2 = 128


def _dft_tables(seq, ch):
    s1n = seq // FFT_S2
    k1 = np.arange(s1n, dtype=np.float64)
    ang1 = 2.0 * np.pi * np.outer(k1, k1) / s1n
    m1 = np.concatenate([np.cos(ang1), -np.sin(ang1)], axis=0)
    s2 = np.arange(FFT_S2, dtype=np.float64)
    kk = k1[:, None, None] + s1n * s2[None, :, None]
    ang2 = 2.0 * np.pi * ((kk * s2[None, None, :]) % seq) / seq
    gc, gs = np.cos(ang2), np.sin(ang2)
    g = np.concatenate([np.concatenate([gc, gs], axis=2),
                        np.concatenate([-gs, gc], axis=2)], axis=1)
    c = np.arange(ch, dtype=np.float64)
    angc = 2.0 * np.pi * (np.outer(c, c) % ch) / ch
    cs = np.concatenate([np.cos(angc), np.sin(angc)], axis=0) / math.sqrt(seq * ch)
    return (jnp.asarray(m1, F32).astype(BF16), jnp.asarray(g, F32).astype(BF16),
            jnp.asarray(cs, F32).astype(BF16))


def _fft_stage1_kernel(m_ref, u_ref, y_ref, uf_ref, yf_ref):
    s1n, sb, tn = u_ref.shape
    chunks = [slice(c * LANES, (c + 1) * LANES) for c in range(tn // LANES)]
    for c, cs in enumerate(chunks):
        uf_ref[c] = u_ref[:, :, cs].astype(F32).reshape(s1n * sb, LANES)
    for j in range(sb):
        rows = jnp.concatenate([uf_ref[c, pl.ds(j, s1n, stride=sb), :] for c in range(len(chunks))], axis=1)
        res = jnp.dot(m_ref[...], rows.astype(BF16), preferred_element_type=F32)
        for c, cs in enumerate(chunks):
            yf_ref[c, pl.ds(j, 2 * s1n, stride=sb), :] = res[:, cs]
    for c, cs in enumerate(chunks):
        y_ref[:, :, cs] = yf_ref[c].reshape(2 * s1n, sb, LANES).astype(y_ref.dtype)


def _fft_stage1(u, m1, *, tn=512):
    b, s1n, _, d = u.shape
    tn = _pick(d, tn)
    sb = BF16_SUBLANES
    return pl.pallas_call(
        _fft_stage1_kernel,
        grid=(b, FFT_S2 // sb, d // tn),
        in_specs=[pl.BlockSpec((2 * s1n, s1n), lambda i, j, c: (0, 0)),
                  pl.BlockSpec((None, s1n, sb, tn), lambda i, j, c: (i, 0, j, c))],
        out_specs=pl.BlockSpec((None, 2 * s1n, sb, tn), lambda i, j, c: (i, 0, j, c)),
        out_shape=jax.ShapeDtypeStruct((b, 2 * s1n, FFT_S2, d), BF16),
        scratch_shapes=[pltpu.VMEM((tn // LANES, s1n * sb, LANES), F32),
                        pltpu.VMEM((tn // LANES, 2 * s1n * sb, LANES), F32)],
        compiler_params=_params(("parallel", "parallel", "parallel")),
        name="fft_stage1",
    )(m1, u)


def _fft_stage2_kernel(g_ref, yr_ref, yi_ref, cs_ref, f_ref, ff_ref):
    kb = g_ref.shape[0]
    for j in range(kb):
        y = jnp.concatenate([yr_ref[j], yi_ref[j]], axis=0)
        xc = jnp.dot(g_ref[j], y, preferred_element_type=F32).astype(BF16)
        xg = jnp.concatenate([xc[:FFT_S2], xc[FFT_S2:]], axis=1)
        ff_ref[:, j, :] = jnp.dot(xg, cs_ref[...], preferred_element_type=F32)
    f_ref[...] = ff_ref[...].astype(f_ref.dtype)


def _fft_stage2(y, g, cs):
    b, s1n2, _, d = y.shape
    s1n = s1n2 // 2
    ch = cs.shape[1]
    kb = BF16_SUBLANES
    assert s1n % kb == 0
    nk = s1n // kb
    return pl.pallas_call(
        _fft_stage2_kernel,
        grid=(b, nk, d // ch),
        in_specs=[pl.BlockSpec((kb, 2 * FFT_S2, 2 * FFT_S2), lambda i, j, q: (j, 0, 0)),
                  pl.BlockSpec((None, kb, FFT_S2, ch), lambda i, j, q: (i, j, 0, q)),
                  pl.BlockSpec((None, kb, FFT_S2, ch), lambda i, j, q: (i, j + nk, 0, q)),
                  pl.BlockSpec(cs.shape, lambda i, j, q: (0, 0))],
        out_specs=pl.BlockSpec((None, FFT_S2, kb, ch), lambda i, j, q: (i, 0, j, q)),
        out_shape=jax.ShapeDtypeStruct((b, FFT_S2, s1n, d), BF16),
        scratch_shapes=[pltpu.VMEM((FFT_S2, kb, ch), F32)],
        compiler_params=_params(("parallel", "parallel", "parallel")),
        name="fft_stage2",
    )(g, y, y, cs)


def _fourier_layer(x, g_norm, w_in, w_out):
    b, s, d = x.shape
    assert s % FFT_S2 == 0 and d % A_GROUPS == 0
    s1n = s // FFT_S2
    m1, g, cs = _dft_tables(s, d // A_GROUPS)
    u = _norm_matmul(x, g_norm, w_in)
    y = _fft_stage1(u.reshape(b, s1n, FFT_S2, d), m1)
    f = _fft_stage2(y, g, cs)
    return [o.reshape(b, s, d) for o in _matmul_res(f.reshape(b * s, d), w_out, x.reshape(b * s, d))]


DIL_TQ = 128
DIL_HALF = 64
DIL_HEAD_GROUP = 4


def _t5_bucket_np(rel):
    nb = T5_BUCKETS // 2
    max_exact = nb // 2
    n = np.abs(rel)
    large = max_exact + (np.log(np.maximum(n, 1) / max_exact) / math.log(T5_MAX_DIST / max_exact)
                         * (nb - max_exact)).astype(np.int32)
    large = np.minimum(large, nb - 1)
    return (np.where(rel > 0, nb, 0) + np.where(n < max_exact, n, large)).astype(np.int32)


def _dilated_bias(t5_bias_g, dilation):
    off = np.arange(DIL_TQ + 2 * DIL_HALF)[None, :] - DIL_HALF - np.arange(DIL_TQ)[:, None]
    band = np.abs(off) <= DIL_HALF
    bias = jnp.take(t5_bias_g, _t5_bucket_np(off * dilation), axis=0).transpose(2, 0, 1)
    return jnp.where(band[None], bias.astype(F32), NEG_INF)


def _dilated_attn_kernel(bias_ref, q_ref, kp_ref, kc_ref, kn_ref, vp_ref, vc_ref, vn_ref,
                         o_ref, l_ref, *, heads, sub_len):
    i = pl.program_id(2)
    nk = DIL_TQ + 2 * DIL_HALF
    kpos = i * DIL_TQ - DIL_HALF + lax.broadcasted_iota(jnp.int32, (1, nk), 1)
    valid = (kpos >= 0) & (kpos < sub_len)
    lane = lax.broadcasted_iota(jnp.int32, (DIL_TQ, LANES), 1)
    lse_all = jnp.zeros((DIL_TQ, LANES), F32)
    scale = HEAD_DIM ** -0.5
    def window(p_ref, c_ref, n_ref, hs):
        return jnp.concatenate([p_ref[DIL_TQ - DIL_HALF:, hs], c_ref[:, hs], n_ref[:DIL_HALF, hs]], axis=0)

    for h0 in range(0, heads, DIL_HEAD_GROUP):
        hss = [slice(h * HEAD_DIM, (h + 1) * HEAD_DIM) for h in range(h0, h0 + DIL_HEAD_GROUP)]
        ss = [lax.dot_general(q_ref[:, hs], window(kp_ref, kc_ref, kn_ref, hs), (((1,), (1,)), ((), ())),
                              preferred_element_type=F32) for hs in hss]
        ps = []
        for j, s in enumerate(ss):
            s = jnp.where(valid, s * scale + bias_ref[h0 + j], NEG_INF)
            m = jnp.max(s, axis=-1, keepdims=True)
            p = jnp.exp(s - m)
            den = jnp.sum(p, axis=-1, keepdims=True)
            ps.append((p * (1.0 / den)).astype(BF16))
            lse_all = jnp.where(lane == h0 + j, m + jnp.log(den), lse_all)
        for p, hs in zip(ps, hss):
            o = jnp.dot(p, window(vp_ref, vc_ref, vn_ref, hs), preferred_element_type=F32)
            o_ref[:, hs] = o.astype(o_ref.dtype)
    l_ref[...] = lse_all


def _dilated_attn(qkv, bias, dilation, heads):
    b, r, sub_len, _ = qkv.shape
    hw = heads * HEAD_DIM
    assert r == dilation and sub_len % DIL_TQ == 0
    nb = sub_len // DIL_TQ

    def spec(c, shift):
        return pl.BlockSpec((None, None, DIL_TQ, hw),
                            lambda bi, rho, i: (bi, rho, jnp.clip(i + shift, 0, nb - 1), c))

    return pl.pallas_call(
        functools.partial(_dilated_attn_kernel, heads=heads, sub_len=sub_len),
        grid=(b, r, nb),
        in_specs=[pl.BlockSpec(bias.shape, lambda bi, rho, i: (0, 0, 0)),
                  spec(0, 0), spec(1, -1), spec(1, 0), spec(1, 1), spec(2, -1), spec(2, 0), spec(2, 1)],
        out_specs=[pl.BlockSpec((None, None, DIL_TQ, hw), lambda bi, rho, i: (bi, rho, i, 0)),
                   pl.BlockSpec((None, None, DIL_TQ, LANES), lambda bi, rho, i: (bi, rho, i, 0))],
        out_shape=[jax.ShapeDtypeStruct((b, r, sub_len, hw), BF16),
                   jax.ShapeDtypeStruct((b, r, sub_len, LANES), F32)],
        compiler_params=_params(("parallel", "parallel", "parallel")),
        name=f"dilated_attn_r{dilation}",
    )(bias, *([qkv] * 7))


def _dilated_combine_kernel(o0_ref, o1_ref, o2_ref, l0_ref, l1_ref, l2_ref, w_ref, x_ref, out_ref, out2_ref,
                            a_ref, oi_ref, li_ref, *, heads):
    for g, (o_ref, l_ref) in enumerate(((o0_ref, l0_ref), (o1_ref, l1_ref), (o2_ref, l2_ref))):
        r, per = o_ref.shape[0], o_ref.shape[1]
        for rho in range(r):
            rows = pl.ds(rho, per, stride=r) if r > 1 else slice(None)
            for h in range(heads):
                oi_ref[g, h, rows, :] = o_ref[rho, :, h * HEAD_DIM:(h + 1) * HEAD_DIM].astype(F32)
            li_ref[g, rows, :] = l_ref[rho]
    l0, l1, l2 = li_ref[0], li_ref[1], li_ref[2]
    m = jnp.maximum(jnp.maximum(l0, l1), l2)
    e0, e1, e2 = jnp.exp(l0 - m), jnp.exp(l1 - m), jnp.exp(l2 - m)
    inv = 1.0 / (e0 + e1 + e2)
    a0, a1, a2 = e0 * inv, e1 * inv, e2 * inv
    for h in range(heads):
        hs = slice(h * HEAD_DIM, (h + 1) * HEAD_DIM)
        o = a0[:, h:h + 1] * oi_ref[0, h] + a1[:, h:h + 1] * oi_ref[1, h] + a2[:, h:h + 1] * oi_ref[2, h]
        a_ref[:, hs] = o.astype(BF16)
    res = x_ref[...] + jnp.dot(a_ref[...], w_ref[...], preferred_element_type=F32)
    out_ref[...] = res
    out2_ref[...] = res


def _dilated_combine(os, ls, w_out, x, *, tm=512):
    b, s, d = x.shape
    hw = w_out.shape[0]
    tm = _pick(s, tm)
    per_b = s // tm

    def dspec(a):
        r, w = a.shape[1], a.shape[3]
        assert tm % (r * BF16_SUBLANES) == 0
        return pl.BlockSpec((None, r, tm // r, w), lambda i: (i // per_b, 0, i % per_b, 0))

    xspec = pl.BlockSpec((None, tm, d), lambda i: (i // per_b, i % per_b, 0))
    return pl.pallas_call(
        functools.partial(_dilated_combine_kernel, heads=hw // HEAD_DIM),
        grid=(b * per_b,),
        in_specs=[dspec(a) for a in os] + [dspec(a) for a in ls] + [pl.BlockSpec((hw, d), lambda i: (0, 0)), xspec],
        out_specs=[xspec, xspec],
        out_shape=[jax.ShapeDtypeStruct((b, s, d), F32)] * 2,
        scratch_shapes=[pltpu.VMEM((tm, hw), BF16), pltpu.VMEM((3, hw // HEAD_DIM, tm, HEAD_DIM), F32),
                        pltpu.VMEM((3, tm, LANES), F32)],
        compiler_params=_params(("parallel",)),
        name="dilated_combine",
    )(*os, *ls, w_out, x)


def _dilated_layer(x, g_norm, w_qkv, w_out, t5_bias):
    hw = w_out.shape[0]
    heads = hw // HEAD_DIM
    assert len(B_PAIRS) == 3 and w_qkv.shape[1] == 9 * hw
    os, ls = [], []
    for g, (window, dilation) in enumerate(B_PAIRS):
        assert window // (2 * dilation) == DIL_HALF
        qkv = _norm_matmul(x, g_norm, w_qkv, col0=g * 3 * hw, n=3 * hw, dilation=dilation)
        bias = _dilated_bias(t5_bias[:, g * heads:(g + 1) * heads], dilation)
        o, l = _dilated_attn(qkv, bias, dilation, heads)
        os.append(o)
        ls.append(l)
    return _dilated_combine(os, ls, w_out, x)


NA_QROWS = 8
NA_KBLK = 4
NA_WIN = NA_QROWS + NA_ROWS
NA_HEAD_GROUP = 4


def _na_bias_table(rpb):
    scale = HEAD_DIM ** -0.5
    qc = np.arange(GRID_W)[None, :]
    kc = np.arange(GRID_W)[:, None]
    cstart = np.clip(qc - NA_COLS // 2, 0, GRID_W - NA_COLS)
    ok = (kc >= cstart) & (kc < cstart + NA_COLS)
    dcol = np.clip(kc - qc, -(NA_COLS - 1), NA_COLS - 1) + NA_COLS - 1
    tab = jnp.take(rpb.astype(F32), dcol, axis=2)
    tab = jnp.where(ok[None, None], tab, NEG_INF) / scale
    hi = tab.astype(BF16)
    lo = (tab - hi.astype(F32)).astype(BF16)
    out = jnp.concatenate([hi, lo], axis=-1)
    return out.reshape(rpb.shape[0], (2 * NA_ROWS - 1) * GRID_W, 2 * GRID_W)


def _na_attn_kernel(bk_ref, q_ref, k0_ref, k1_ref, k2_ref, k3_ref, v0_ref, v1_ref, v2_ref, v3_ref,
                    o_ref, kw_ref, vw_ref, *, heads, rows):
    rb = pl.program_id(1)
    blk = NA_KBLK * GRID_W
    for j, (kr, vr) in enumerate(((k0_ref, v0_ref), (k1_ref, v1_ref), (k2_ref, v2_ref), (k3_ref, v3_ref))):
        kw_ref[j * blk:(j + 1) * blk, :] = kr[...]
        vw_ref[j * blk:(j + 1) * blk, :] = vr[...]
    ws = NA_KBLK * jnp.clip(2 * rb - 1, 0, rows // NA_KBLK - NA_WIN // NA_KBLK)
    nkeys = NA_ROWS * GRID_W
    eye = (lax.broadcasted_iota(jnp.int32, (GRID_W, 2 * GRID_W), 0)
           == lax.broadcasted_iota(jnp.int32, (GRID_W, 2 * GRID_W), 1) % GRID_W)
    eye2 = jnp.where(eye, 1.0, 0.0).astype(BF16)
    scale = HEAD_DIM ** -0.5
    nt = (((1,), (1,)), ((), ()))

    def row_body(rr, carry):
        r = rb * NA_QROWS + rr
        rs = jnp.clip(r - NA_ROWS // 2, 0, rows - NA_ROWS)
        ksl = pl.ds(pl.multiple_of((rs - ws) * GRID_W, GRID_W), nkeys)
        bsl = pl.ds(pl.multiple_of((rs - r + NA_ROWS - 1) * GRID_W, GRID_W), nkeys)
        qsl = pl.ds(pl.multiple_of(rr * GRID_W, GRID_W), GRID_W)
        for h0 in range(0, heads, NA_HEAD_GROUP):
            hss = [slice(h * HEAD_DIM, (h + 1) * HEAD_DIM) for h in range(h0, h0 + NA_HEAD_GROUP)]
            ss = [lax.dot_general(q_ref[qsl, hs], kw_ref[ksl, hs], nt, preferred_element_type=F32)
                  + lax.dot_general(eye2, bk_ref[h0 + j, bsl, :], nt, preferred_element_type=F32)
                  for j, hs in enumerate(hss)]
            ps = []
            for s in ss:
                s = s * scale
                p = jnp.exp(s - jnp.max(s, axis=-1, keepdims=True))
                ps.append((p * (1.0 / jnp.sum(p, axis=-1, keepdims=True))).astype(BF16))
            for p, hs in zip(ps, hss):
                o_ref[qsl, hs] = jnp.dot(p, vw_ref[ksl, hs], preferred_element_type=F32).astype(o_ref.dtype)
        return carry

    lax.fori_loop(0, NA_QROWS, row_body, 0)


def _na_attn(qkv, bk, heads):
    b, s, _ = qkv.shape
    hw = heads * HEAD_DIM
    rows = s // GRID_W
    assert s % GRID_W == 0 and rows % NA_QROWS == 0 and rows >= NA_WIN and 2 * GRID_W == LANES
    assert heads % NA_HEAD_GROUP == 0
    nwb = rows // NA_KBLK - NA_WIN // NA_KBLK
    blk = NA_KBLK * GRID_W

    def kv_spec(c, j):
        return pl.BlockSpec((None, blk, hw), lambda bi, rb: (bi, jnp.clip(2 * rb - 1, 0, nwb) + j, c))

    o = pl.pallas_call(
        functools.partial(_na_attn_kernel, heads=heads, rows=rows),
        grid=(b, rows // NA_QROWS),
        in_specs=[pl.BlockSpec(bk.shape, lambda bi, rb: (0, 0, 0)),
                  pl.BlockSpec((None, NA_QROWS * GRID_W, hw), lambda bi, rb: (bi, rb, 0))]
                 + [kv_spec(1, j) for j in range(4)] + [kv_spec(2, j) for j in range(4)],
        out_specs=pl.BlockSpec((None, NA_QROWS * GRID_W, hw), lambda bi, rb: (bi, rb, 0)),
        out_shape=jax.ShapeDtypeStruct((b, s, hw), BF16),
        scratch_shapes=[pltpu.VMEM((NA_WIN * GRID_W, hw), BF16), pltpu.VMEM((NA_WIN * GRID_W, hw), BF16)],
        compiler_params=_params(("parallel", "parallel")),
        name="na_attn",
    )(bk, *([qkv] * 9))
    return o.reshape(b * s, hw)


def _neighbourhood_layer(x, g_norm, w_qkv, w_out, rpb):
    b, s, d = x.shape
    hw = w_out.shape[0]
    assert w_qkv.shape[1] == 3 * hw
    qkv = _norm_matmul(x, g_norm, w_qkv).reshape(b, s, 3 * hw)
    o = _na_attn(qkv, _na_bias_table(rpb), hw // HEAD_DIM)
    return [y.reshape(b, s, d) for y in _matmul_res(o, w_out, x.reshape(b * s, d))]


def _router_kernel(x_ref, g_ref, wt_ref, aff_ref):
    hn = _rmsnorm_rows(x_ref[...], g_ref[...]).astype(BF16)
    logits = lax.dot_general(wt_ref[...], hn, (((1,), (1,)), ((), ())), preferred_element_type=F32)
    m = jnp.max(logits, axis=0, keepdims=True)
    p = jnp.exp(logits - m)
    aff_ref[...] = p / jnp.sum(p, axis=0, keepdims=True)


def _router(x, g, w_router_t, *, tm=512):
    t, d = x.shape
    e = w_router_t.shape[0]
    tm = _pick(t, tm)
    return pl.pallas_call(
        _router_kernel,
        grid=(t // tm,),
        in_specs=[pl.BlockSpec((tm, d), lambda i: (i, 0)),
                  pl.BlockSpec((1, d), lambda i: (0, 0)),
                  pl.BlockSpec((e, d), lambda i: (0, 0))],
        out_specs=pl.BlockSpec((e, tm), lambda i: (0, i)),
        out_shape=jax.ShapeDtypeStruct((e, t), F32),
        compiler_params=_params(("parallel",)),
        name="moe_router",
    )(x, g.reshape(1, d), w_router_t)


def _topk_mask_kernel(aff_ref, sel_ref, *, cap):
    bits = pltpu.bitcast(aff_ref[...], jnp.int32)
    e, t = bits.shape

    def count(mask):
        return jnp.sum(jnp.where(mask, 1, 0), axis=1, keepdims=True)

    def value_step(_, c):
        lo, hi = c
        mid = lo + (hi - lo) // 2
        ok = count(bits >= mid) >= cap
        return jnp.where(ok, mid, lo), jnp.where(ok, hi, mid)

    lo0 = jnp.zeros((e, 1), jnp.int32)
    hi0 = jnp.full((e, 1), 0x7F800000, jnp.int32)
    thr, _ = lax.fori_loop(0, 31, value_step, (lo0, hi0))
    above = bits > thr
    tie = bits == thr
    need = cap - count(above)
    tok = lax.broadcasted_iota(jnp.int32, (e, t), 1)

    def index_step(_, c):
        lo, hi = c
        mid = lo + (hi - lo) // 2
        ok = count(tie & (tok < mid)) >= need
        return jnp.where(ok, lo, mid), jnp.where(ok, mid, hi)

    _, bound = lax.fori_loop(0, max(1, math.ceil(math.log2(t))), index_step,
                             (jnp.zeros((e, 1), jnp.int32), jnp.full((e, 1), t, jnp.int32)))
    sel_ref[...] = jnp.where(above | (tie & (tok < bound)), 1, 0)


def _topk_mask(aff_t, cap):
    e, t = aff_t.shape
    return pl.pallas_call(
        functools.partial(_topk_mask_kernel, cap=cap),
        grid=(1,),
        in_specs=[pl.BlockSpec((e, t), lambda i: (0, 0))],
        out_specs=pl.BlockSpec((e, t), lambda i: (0, 0)),
        out_shape=jax.ShapeDtypeStruct((e, t), jnp.int32),
        compiler_params=_params(("arbitrary",)),
        name="moe_topk_mask",
    )(aff_t)


def _compact_kernel(sel_ref, aff_ref, idx_ref, gate_ref, cnt_ref, bound_ref, piece_ref, *, tj):
    nb = sel_ref.shape[0]

    @pl.when(pl.program_id(1) == 0)
    def _():
        m = sel_ref[...].astype(F32).astype(BF16)
        tri = jnp.where(lax.broadcasted_iota(jnp.int32, (LANES, LANES), 0)
                        <= lax.broadcasted_iota(jnp.int32, (LANES, LANES), 1), 1.0, 0.0).astype(BF16)
        cnt_ref[...] = jnp.dot(m, tri, preferred_element_type=F32).astype(BF16)
        tot = lax.dot_general(jnp.ones((8, LANES), BF16), m, (((1,), (1,)), ((), ())),
                              preferred_element_type=F32)
        trib = jnp.where(lax.broadcasted_iota(jnp.int32, (nb, nb), 0)
                         <= lax.broadcasted_iota(jnp.int32, (nb, nb), 1), 1.0, 0.0).astype(BF16)
        incl = jnp.dot(tot.astype(BF16), trib, preferred_element_type=F32)
        bound_ref[0:8, :] = incl
        bound_ref[8:16, :] = incl - tot
        bits = pltpu.bitcast(aff_ref[...], jnp.int32)
        for k in range(4):
            piece_ref[k] = ((bits >> (8 * k)) & 0xFF).astype(F32).astype(BF16)

    jf = (pl.program_id(1) * tj + lax.broadcasted_iota(jnp.int32, (tj, 1), 0)).astype(F32)
    blk = jnp.sum(jnp.where(bound_ref[0:1, :] <= jf, 1, 0), axis=1, keepdims=True)
    onehot = lax.broadcasted_iota(jnp.int32, (tj, nb), 1) == blk
    start = jnp.sum(jnp.where(onehot, bound_ref[8:9, :], 0.0), axis=1, keepdims=True)
    oh = jnp.where(onehot, 1.0, 0.0).astype(BF16)
    cnt = jnp.dot(oh, cnt_ref[...], preferred_element_type=F32)
    lane = jnp.sum(jnp.where(cnt < jf - start + 1.0, 1, 0), axis=1, keepdims=True)
    idx_ref[...] = blk * LANES + lane
    pick = lax.broadcasted_iota(jnp.int32, (tj, LANES), 1) == lane
    bits = jnp.zeros((tj, 1), jnp.int32)
    for k in range(4):
        byte = jnp.dot(oh, piece_ref[k], preferred_element_type=F32)
        bits = bits | (jnp.sum(jnp.where(pick, byte, 0.0), axis=1, keepdims=True).astype(jnp.int32) << (8 * k))
    gate_ref[...] = pltpu.bitcast(bits, F32)


def _compact(sel, aff_t, cap, *, tj=512):
    e, t = sel.shape
    assert t % LANES == 0
    nb = t // LANES
    tj = _pick(cap, tj)
    nj = cap // tj
    blocks = pl.BlockSpec((None, nb, LANES), lambda ei, j: (ei, 0, 0))
    col = pl.BlockSpec((tj, 1), lambda ei, j: (ei * nj + j, 0))
    return pl.pallas_call(
        functools.partial(_compact_kernel, tj=tj),
        grid=(e, nj),
        in_specs=[blocks, blocks],
        out_specs=[col, col],
        out_shape=[jax.ShapeDtypeStruct((e * cap, 1), jnp.int32), jax.ShapeDtypeStruct((e * cap, 1), F32)],
        scratch_shapes=[pltpu.VMEM((nb, LANES), BF16), pltpu.VMEM((16, nb), F32), pltpu.VMEM((4, nb, LANES), BF16)],
        compiler_params=_params(("parallel", "arbitrary")),
        name="moe_compact",
    )(sel.reshape(e, nb, LANES), aff_t.reshape(e, nb, LANES))


X_SEM, OUT_SEM, SCATTER_SEM = 0, 1, 2


def _moe_ffn_kernel(idx_ref, gate_ref, gn_ref, wg_ref, wu_ref, wd_ref, x_hbm, yin_hbm, out_hbm,
                    xg_ref, ob_ref, h_ref, acc_ref, sem, *, tm, nt, nf, n_tiles):
    del yin_hbm
    i = pl.program_id(1)
    tile = pl.program_id(0) * nt + i
    base = tile * tm
    f = pl.program_id(2)
    slot = tile % 2
    first_of_expert = i == 0
    after_first = i == 1 % nt

    def row_copy(hbm, buf, first, r, s, to_vmem):
        tok = idx_ref[first + r]
        src, dst = hbm.at[pl.ds(tok, 1), :], buf.at[pl.ds(r, 1), :]
        return (pltpu.make_async_copy(src, dst, sem.at[s]) if to_vmem
                else pltpu.make_async_copy(dst, src, sem.at[s]))

    def start_rows(hbm, buf, first, s, to_vmem):
        def body(r, c):
            row_copy(hbm, buf, first, r, s, to_vmem).start()
            return c
        lax.fori_loop(0, tm, body, 0, unroll=8)

    def wait_rows(hbm, buf, s, to_vmem):
        whole = hbm.at[pl.ds(0, tm), :]
        (pltpu.make_async_copy(whole, buf, sem.at[s]) if to_vmem
         else pltpu.make_async_copy(buf, whole, sem.at[s])).wait()

    def wait_scatter(s):
        wait_rows(out_hbm, ob_ref.at[s], SCATTER_SEM + s, False)

    @pl.when(f == 0)
    def _():
        @pl.when(tile == 0)
        def _():
            start_rows(x_hbm, xg_ref, base, X_SEM, True)

        @pl.when((tile >= 2) & jnp.logical_not(after_first))
        def _():
            wait_scatter(slot)

        @pl.when((tile >= 1) & first_of_expert)
        def _():
            wait_scatter(1 - slot)

        wait_rows(x_hbm, xg_ref, X_SEM, True)
        h_ref[...] = _rmsnorm_rows(xg_ref[...], gn_ref[...]).astype(BF16)
        acc_ref[...] = jnp.zeros_like(acc_ref)

    nxt = jnp.minimum(tile + 1, n_tiles - 1) * tm
    per_step = tm // nf
    for r in range(per_step):
        row = f * per_step + r
        row_copy(out_hbm, ob_ref.at[slot], base, row, OUT_SEM, True).start()
        row_copy(x_hbm, xg_ref, nxt, row, X_SEM, True).start()

    h = h_ref[...]
    g = jnp.dot(h, wg_ref[...].astype(BF16), preferred_element_type=F32)
    u = jnp.dot(h, wu_ref[...].astype(BF16), preferred_element_type=F32)
    hid = (g * (1.0 / (1.0 + jnp.exp(-g))) * u).astype(BF16)
    acc_ref[...] += jnp.dot(hid, wd_ref[...].astype(BF16), preferred_element_type=F32)

    @pl.when(f == nf - 1)
    def _():
        wait_rows(out_hbm, ob_ref.at[slot], OUT_SEM, True)
        ob_ref[slot] = ob_ref[slot] + acc_ref[...] * gate_ref[...]
        start_rows(out_hbm, ob_ref.at[slot], base, SCATTER_SEM + slot, False)

        @pl.when(tile == n_tiles - 1)
        def _():
            @pl.when((tile >= 1) & jnp.logical_not(first_of_expert))
            def _():
                wait_scatter(1 - slot)

            wait_scatter(slot)
            wait_rows(x_hbm, xg_ref, X_SEM, True)


def _moe_ffn(x, x_copy, idx, gate, n_experts, g_norm, wg, wu, wd, layer, *, tm=1024, tf=256):
    t, d = x.shape
    e = n_experts
    cap = idx.shape[0] // e
    fdim = wg.shape[3]
    tm = _pick(cap, tm)
    tf = _pick(fdim, tf)
    nt, nf = cap // tm, fdim // tf
    assert tm % nf == 0
    grid_spec = pltpu.PrefetchScalarGridSpec(
        num_scalar_prefetch=1,
        grid=(e, nt, nf),
        in_specs=[pl.BlockSpec((tm, 1), lambda ei, i, f, idx: (ei * nt + i, 0)),
                  pl.BlockSpec((1, d), lambda ei, i, f, idx: (0, 0)),
                  pl.BlockSpec((None, None, d, tf), lambda ei, i, f, idx: (layer, ei, 0, f)),
                  pl.BlockSpec((None, None, d, tf), lambda ei, i, f, idx: (layer, ei, 0, f)),
                  pl.BlockSpec((None, None, tf, d), lambda ei, i, f, idx: (layer, ei, f, 0)),
                  pl.BlockSpec(memory_space=pl.ANY),
                  pl.BlockSpec(memory_space=pl.ANY)],
        out_specs=pl.BlockSpec(memory_space=pl.ANY),
        scratch_shapes=[pltpu.VMEM((tm, d), F32), pltpu.VMEM((2, tm, d), F32), pltpu.VMEM((tm, d), BF16),
                        pltpu.VMEM((tm, d), F32), pltpu.SemaphoreType.DMA((4,))],
    )
    return pl.pallas_call(
        functools.partial(_moe_ffn_kernel, tm=tm, nt=nt, nf=nf, n_tiles=e * nt),
        grid_spec=grid_spec,
        out_shape=jax.ShapeDtypeStruct((t, d), F32),
        input_output_aliases={7: 0},
        compiler_params=_params(("arbitrary", "arbitrary", "arbitrary")),
        name="moe_ffn",
    )(idx.reshape(e * cap), gate, g_norm.reshape(1, d), wg, wu, wd, x, x_copy)


def _moe_layer(x, x_copy, g_norm, w_router, wg, wu, wd, layer):
    b, s, d = x.shape
    t = b * s
    e = w_router.shape[1]
    cap = EC_CAPACITY * t // e
    xt = x.reshape(t, d)
    aff_t = _router(xt, g_norm, w_router.T.astype(BF16))
    sel = _topk_mask(aff_t, cap)
    idx, gate = _compact(sel, aff_t, cap)
    return _moe_ffn(xt, x_copy.reshape(t, d), idx, gate, e, g_norm, wg, wu, wd, layer).reshape(b, s, d)


def _trunk(x, w):
    for i in range(w["norm_mix"].shape[0]):
        kind, j = i % N_MIXERS, i // N_MIXERS
        if kind == 0:
            x, x2 = _fourier_layer(x, w["norm_mix"][i], w["a_w_in"][j], w["a_w_out"][j])
        elif kind == 1:
            x, x2 = _dilated_layer(x, w["norm_mix"][i], w["b_w_qkv"][j], w["b_w_out"][j], w["t5_bias"])
        else:
            x, x2 = _neighbourhood_layer(x, w["norm_mix"][i], w["c_w_qkv"][j], w["c_w_out"][j], w["c_rpb"][j])
        x = _moe_layer(x, x2, w["norm_ffn"][i], w["moe_router"][i], w["moe_w_gate"], w["moe_w_up"],
                       w["moe_w_down"], i)
    b, s, d = x.shape
    return _rmsnorm(x.reshape(b * s, d), w["norm_final"]).reshape(b, s, d)


def kernel(x_prompt, x_sample, norm_mix, norm_ffn, norm_final, a_w_in, a_w_out, b_w_qkv, b_w_out, t5_bias,
           c_w_qkv, c_w_out, c_rpb, moe_router, moe_w_gate, moe_w_up, moe_w_down):
    bf = lambda a: a.astype(BF16)
    w = dict(norm_mix=norm_mix, norm_ffn=norm_ffn, norm_final=norm_final, a_w_in=bf(a_w_in), a_w_out=bf(a_w_out),
             b_w_qkv=bf(b_w_qkv), b_w_out=bf(b_w_out), t5_bias=t5_bias, c_w_qkv=bf(c_w_qkv), c_w_out=bf(c_w_out),
             c_rpb=c_rpb, moe_router=moe_router, moe_w_gate=moe_w_gate, moe_w_up=moe_w_up, moe_w_down=moe_w_down)
    return _trunk(x_prompt, w), _trunk(x_sample, w)
```

```python
import functools
import math

import numpy as np
import jax
import jax.numpy as jnp
from jax import lax
from jax.experimental import pallas as pl
from jax.experimental.pallas import tpu as pltpu

F32 = jnp.float32
BF16 = jnp.bfloat16

RMS_EPS = 1e-6
NEG_INF = -1e30
HEAD_DIM = 128
GRID_W = 64
A_GROUPS = 4
B_PAIRS = ((128, 1), (512, 4), (2048, 16))
T5_BUCKETS = 32
T5_MAX_DIST = 1024
NA_ROWS = 8
NA_COLS = 16
EC_CAPACITY = 2
N_MIXERS = 3

V7X_VMEM_BYTES = 64 * 1024 * 1024
VMEM_LIMIT = V7X_VMEM_BYTES - 8 * 1024 * 1024
LANES = 128
BF16_SUBLANES = 16


def _params(sem):
    return pltpu.CompilerParams(dimension_semantics=sem, vmem_limit_bytes=VMEM_LIMIT)


def _pick(n, pref):
    t = min(n, pref)
    while n % t:
        t //= 2
    return t


def _rmsnorm_rows(x, g):
    ms = jnp.mean(x * x, axis=-1, keepdims=True)
    return (x * lax.rsqrt(ms + RMS_EPS)) * g


def _norm_matmul_kernel(x_ref, g_ref, w_ref, o_ref, h_ref, r_ref, *, rows, dilation):
    @pl.when(pl.program_id(1) == 0)
    def _():
        def body(c, carry):
            sl = pl.ds(pl.multiple_of(c * rows, rows), rows)
            h_ref[sl, :] = _rmsnorm_rows(x_ref[sl, :], g_ref[...]).astype(BF16)
            return carry
        lax.fori_loop(0, x_ref.shape[0] // rows, body, 0)

    res = jnp.dot(h_ref[...], w_ref[...], preferred_element_type=F32)
    if dilation == 1:
        o_ref[0] = res.astype(o_ref.dtype)
    else:
        chunks = [slice(c * LANES, (c + 1) * LANES) for c in range(r_ref.shape[0])]
        for c, cs in enumerate(chunks):
            r_ref[c] = res[:, cs]
        per = r_ref.shape[1] // dilation
        for rho in range(dilation):
            for c, cs in enumerate(chunks):
                o_ref[rho, :, cs] = r_ref[c, pl.ds(rho, per, stride=dilation), :].astype(o_ref.dtype)


def _norm_matmul(x, g, w, *, col0=0, n=None, dilation=1, tm=1024, tn=1024):
    b, s, d = x.shape
    n = w.shape[1] if n is None else n
    tm = _pick(s, tm)
    tn = _pick(n, tn)
    assert col0 % tn == 0 and tm % (dilation * BF16_SUBLANES) == 0
    rows = _pick(tm, 256)
    per_b = s // tm
    cb0 = col0 // tn
    out = pl.pallas_call(
        functools.partial(_norm_matmul_kernel, rows=rows, dilation=dilation),
        grid=(b * per_b, n // tn),
        in_specs=[pl.BlockSpec((None, tm, d), lambda i, j: (i // per_b, i % per_b, 0)),
                  pl.BlockSpec((1, d), lambda i, j: (0, 0)),
                  pl.BlockSpec((d, tn), lambda i, j: (0, cb0 + j))],
        out_specs=pl.BlockSpec((None, dilation, tm // dilation, tn), lambda i, j: (i // per_b, 0, i % per_b, j)),
        out_shape=jax.ShapeDtypeStruct((b, dilation, s // dilation, n), BF16),
        scratch_shapes=[pltpu.VMEM((tm, d), BF16),
                        pltpu.VMEM((tn // LANES, tm, LANES) if dilation > 1 else (1, 8, LANES), F32)],
        compiler_params=_params(("parallel", "arbitrary")),
        name="norm_matmul",
    )(x, g.reshape(1, d), w)
    return out


def _matmul_res_kernel(a_ref, w_ref, x_ref, o_ref, o2_ref):
    res = x_ref[...] + jnp.dot(a_ref[...], w_ref[...], preferred_element_type=F32)
    o_ref[...] = res
    o2_ref[...] = res


def _matmul_res(a, w, x, *, tm=512):
    t, k = a.shape
    d = w.shape[1]
    tm = _pick(t, tm)
    row = pl.BlockSpec((tm, d), lambda i: (i, 0))
    return pl.pallas_call(
        _matmul_res_kernel,
        grid=(t // tm,),
        in_specs=[pl.BlockSpec((tm, k), lambda i: (i, 0)), pl.BlockSpec((k, d), lambda i: (0, 0)), row],
        out_specs=[row, row],
        out_shape=[jax.ShapeDtypeStruct((t, d), F32)] * 2,
        compiler_params=_params(("parallel",)),
        name="matmul_res",
    )(a, w, x)


def _rmsnorm_kernel(x_ref, g_ref, o_ref):
    o_ref[...] = _rmsnorm_rows(x_ref[...], g_ref[...])


def _rmsnorm(x, g, *, tm=256):
    t, d = x.shape
    tm = _pick(t, tm)
    return pl.pallas_call(
        _rmsnorm_kernel,
        grid=(t // tm,),
        in_specs=[pl.BlockSpec((tm, d), lambda i: (i, 0)),
                  pl.BlockSpec((1, d), lambda i: (0, 0))],
        out_specs=pl.BlockSpec((tm, d), lambda i: (i, 0)),
        out_shape=jax.ShapeDtypeStruct((t, d), F32),
        compiler_params=_params(("parallel",)),
        name="final_rmsnorm",
    )(x, g.reshape(1, d))


FFT_S2 = 128


def _dft_tables(seq, ch):
    s1n = seq // FFT_S2
    k1 = np.arange(s1n, dtype=np.float64)
    ang1 = 2.0 * np.pi * np.outer(k1, k1) / s1n
    m1 = np.concatenate([np.cos(ang1), -np.sin(ang1)], axis=0)
    s2 = np.arange(FFT_S2, dtype=np.float64)
    kk = k1[:, None, None] + s1n * s2[None, :, None]
    ang2 = 2.0 * np.pi * ((kk * s2[None, None, :]) % seq) / seq
    gc, gs = np.cos(ang2), np.sin(ang2)
    g = np.concatenate([np.concatenate([gc, gs], axis=2),
                        np.concatenate([-gs, gc], axis=2)], axis=1)
    c = np.arange(ch, dtype=np.float64)
    angc = 2.0 * np.pi * (np.outer(c, c) % ch) / ch
    cs = np.concatenate([np.cos(angc), np.sin(angc)], axis=0) / math.sqrt(seq * ch)
    return (jnp.asarray(m1, F32).astype(BF16), jnp.asarray(g, F32).astype(BF16),
            jnp.asarray(cs, F32).astype(BF16))


def _fft_stage1_kernel(m_ref, u_ref, y_ref, uf_ref, yf_ref):
    s1n, sb, tn = u_ref.shape
    chunks = [slice(c * LANES, (c + 1) * LANES) for c in range(tn // LANES)]
    for c, cs in enumerate(chunks):
        uf_ref[c] = u_ref[:, :, cs].astype(F32).reshape(s1n * sb, LANES)
    for j in range(sb):
        rows = jnp.concatenate([uf_ref[c, pl.ds(j, s1n, stride=sb), :] for c in range(len(chunks))], axis=1)
        res = jnp.dot(m_ref[...], rows.astype(BF16), preferred_element_type=F32)
        for c, cs in enumerate(chunks):
            yf_ref[c, pl.ds(j, 2 * s1n, stride=sb), :] = res[:, cs]
    for c, cs in enumerate(chunks):
        y_ref[:, :, cs] = yf_ref[c].reshape(2 * s1n, sb, LANES).astype(y_ref.dtype)


def _fft_stage1(u, m1, *, tn=512):
    b, s1n, _, d = u.shape
    tn = _pick(d, tn)
    sb = BF16_SUBLANES
    return pl.pallas_call(
        _fft_stage1_kernel,
        grid=(b, FFT_S2 // sb, d // tn),
        in_specs=[pl.BlockSpec((2 * s1n, s1n), lambda i, j, c: (0, 0)),
                  pl.BlockSpec((None, s1n, sb, tn), lambda i, j, c: (i, 0, j, c))],
        out_specs=pl.BlockSpec((None, 2 * s1n, sb, tn), lambda i, j, c: (i, 0, j, c)),
        out_shape=jax.ShapeDtypeStruct((b, 2 * s1n, FFT_S2, d), BF16),
        scratch_shapes=[pltpu.VMEM((tn // LANES, s1n * sb, LANES), F32),
                        pltpu.VMEM((tn // LANES, 2 * s1n * sb, LANES), F32)],
        compiler_params=_params(("parallel", "parallel", "parallel")),
        name="fft_stage1",
    )(m1, u)


def _fft_stage2_kernel(g_ref, yr_ref, yi_ref, cs_ref, f_ref, ff_ref):
    kb = g_ref.shape[0]
    for j in range(kb):
        y = jnp.concatenate([yr_ref[j], yi_ref[j]], axis=0)
        xc = jnp.dot(g_ref[j], y, preferred_element_type=F32).astype(BF16)
        xg = jnp.concatenate([xc[:FFT_S2], xc[FFT_S2:]], axis=1)
        ff_ref[:, j, :] = jnp.dot(xg, cs_ref[...], preferred_element_type=F32)
    f_ref[...] = ff_ref[...].astype(f_ref.dtype)


def _fft_stage2(y, g, cs):
    b, s1n2, _, d = y.shape
    s1n = s1n2 // 2
    ch = cs.shape[1]
    kb = BF16_SUBLANES
    assert s1n % kb == 0
    nk = s1n // kb
    return pl.pallas_call(
        _fft_stage2_kernel,
        grid=(b, nk, d // ch),
        in_specs=[pl.BlockSpec((kb, 2 * FFT_S2, 2 * FFT_S2), lambda i, j, q: (j, 0, 0)),
                  pl.BlockSpec((None, kb, FFT_S2, ch), lambda i, j, q: (i, j, 0, q)),
                  pl.BlockSpec((None, kb, FFT_S2, ch), lambda i, j, q: (i, j + nk, 0, q)),
                  pl.BlockSpec(cs.shape, lambda i, j, q: (0, 0))],
        out_specs=pl.BlockSpec((None, FFT_S2, kb, ch), lambda i, j, q: (i, 0, j, q)),
        out_shape=jax.ShapeDtypeStruct((b, FFT_S2, s1n, d), BF16),
        scratch_shapes=[pltpu.VMEM((FFT_S2, kb, ch), F32)],
        compiler_params=_params(("parallel", "parallel", "parallel")),
        name="fft_stage2",
    )(g, y, y, cs)


def _fourier_layer(x, g_norm, w_in, w_out):
    b, s, d = x.shape
    assert s % FFT_S2 == 0 and d % A_GROUPS == 0
    s1n = s // FFT_S2
    m1, g, cs = _dft_tables(s, d // A_GROUPS)
    u = _norm_matmul(x, g_norm, w_in)
    y = _fft_stage1(u.reshape(b, s1n, FFT_S2, d), m1)
    f = _fft_stage2(y, g, cs)
    return [o.reshape(b, s, d) for o in _matmul_res(f.reshape(b * s, d), w_out, x.reshape(b * s, d))]


DIL_TQ = 128
DIL_HALF = 64
DIL_HEAD_GROUP = 4


def _lookup_rows(table, index):
    onehot = jnp.asarray(np.eye(table.shape[0], dtype=np.float32)[index])
    return jnp.tensordot(onehot, table.astype(F32), axes=1, precision=lax.Precision.HIGHEST)


def _t5_bucket_np(rel):
    nb = T5_BUCKETS // 2
    max_exact = nb // 2
    n = np.abs(rel)
    large = max_exact + (np.log(np.maximum(n, 1) / max_exact) / math.log(T5_MAX_DIST / max_exact)
                         * (nb - max_exact)).astype(np.int32)
    large = np.minimum(large, nb - 1)
    return (np.where(rel > 0, nb, 0) + np.where(n < max_exact, n, large)).astype(np.int32)


def _dilated_bias(t5_bias_g, dilation):
    off = np.arange(DIL_TQ + 2 * DIL_HALF)[None, :] - DIL_HALF - np.arange(DIL_TQ)[:, None]
    band = np.abs(off) <= DIL_HALF
    bias = _lookup_rows(t5_bias_g, _t5_bucket_np(off * dilation)).transpose(2, 0, 1)
    return jnp.where(band[None], bias.astype(F32), NEG_INF)


def _dilated_attn_kernel(bias_ref, q_ref, kp_ref, kc_ref, kn_ref, vp_ref, vc_ref, vn_ref,
                         o_ref, l_ref, *, heads, sub_len):
    i = pl.program_id(2)
    nk = DIL_TQ + 2 * DIL_HALF
    kpos = i * DIL_TQ - DIL_HALF + lax.broadcasted_iota(jnp.int32, (1, nk), 1)
    valid = (kpos >= 0) & (kpos < sub_len)
    lane = lax.broadcasted_iota(jnp.int32, (DIL_TQ, LANES), 1)
    lse_all = jnp.zeros((DIL_TQ, LANES), F32)
    scale = HEAD_DIM ** -0.5
    def window(p_ref, c_ref, n_ref, hs):
        return jnp.concatenate([p_ref[DIL_TQ - DIL_HALF:, hs], c_ref[:, hs], n_ref[:DIL_HALF, hs]], axis=0)

    for h0 in range(0, heads, DIL_HEAD_GROUP):
        hss = [slice(h * HEAD_DIM, (h + 1) * HEAD_DIM) for h in range(h0, h0 + DIL_HEAD_GROUP)]
        ss = [lax.dot_general(q_ref[:, hs], window(kp_ref, kc_ref, kn_ref, hs), (((1,), (1,)), ((), ())),
                              preferred_element_type=F32) for hs in hss]
        ps = []
        for j, s in enumerate(ss):
            s = jnp.where(valid, s * scale + bias_ref[h0 + j], NEG_INF)
            m = jnp.max(s, axis=-1, keepdims=True)
            p = jnp.exp(s - m)
            den = jnp.sum(p, axis=-1, keepdims=True)
            ps.append((p * (1.0 / den)).astype(BF16))
            lse_all = jnp.where(lane == h0 + j, m + jnp.log(den), lse_all)
        for p, hs in zip(ps, hss):
            o = jnp.dot(p, window(vp_ref, vc_ref, vn_ref, hs), preferred_element_type=F32)
            o_ref[:, hs] = o.astype(o_ref.dtype)
    l_ref[...] = lse_all


def _dilated_attn(qkv, bias, dilation, heads):
    b, r, sub_len, _ = qkv.shape
    hw = heads * HEAD_DIM
    assert r == dilation and sub_len % DIL_TQ == 0
    nb = sub_len // DIL_TQ

    def spec(c, shift):
        return pl.BlockSpec((None, None, DIL_TQ, hw),
                            lambda bi, rho, i: (bi, rho, jnp.clip(i + shift, 0, nb - 1), c))

    return pl.pallas_call(
        functools.partial(_dilated_attn_kernel, heads=heads, sub_len=sub_len),
        grid=(b, r, nb),
        in_specs=[pl.BlockSpec(bias.shape, lambda bi, rho, i: (0, 0, 0)),
                  spec(0, 0), spec(1, -1), spec(1, 0), spec(1, 1), spec(2, -1), spec(2, 0), spec(2, 1)],
        out_specs=[pl.BlockSpec((None, None, DIL_TQ, hw), lambda bi, rho, i: (bi, rho, i, 0)),
                   pl.BlockSpec((None, None, DIL_TQ, LANES), lambda bi, rho, i: (bi, rho, i, 0))],
        out_shape=[jax.ShapeDtypeStruct((b, r, sub_len, hw), BF16),
                   jax.ShapeDtypeStruct((b, r, sub_len, LANES), F32)],
        compiler_params=_params(("parallel", "parallel", "parallel")),
        name=f"dilated_attn_r{dilation}",
    )(bias, *([qkv] * 7))


def _dilated_combine_kernel(o0_ref, o1_ref, o2_ref, l0_ref, l1_ref, l2_ref, w_ref, x_ref, out_ref, out2_ref,
                            a_ref, oi_ref, li_ref, *, heads):
    for g, (o_ref, l_ref) in enumerate(((o0_ref, l0_ref), (o1_ref, l1_ref), (o2_ref, l2_ref))):
        r, per = o_ref.shape[0], o_ref.shape[1]
        for rho in range(r):
            rows = pl.ds(rho, per, stride=r) if r > 1 else slice(None)
            for h in range(heads):
                oi_ref[g, h, rows, :] = o_ref[rho, :, h * HEAD_DIM:(h + 1) * HEAD_DIM].astype(F32)
            li_ref[g, rows, :] = l_ref[rho]
    l0, l1, l2 = li_ref[0], li_ref[1], li_ref[2]
    m = jnp.maximum(jnp.maximum(l0, l1), l2)
    e0, e1, e2 = jnp.exp(l0 - m), jnp.exp(l1 - m), jnp.exp(l2 - m)
    inv = 1.0 / (e0 + e1 + e2)
    a0, a1, a2 = e0 * inv, e1 * inv, e2 * inv
    for h in range(heads):
        hs = slice(h * HEAD_DIM, (h + 1) * HEAD_DIM)
        o = a0[:, h:h + 1] * oi_ref[0, h] + a1[:, h:h + 1] * oi_ref[1, h] + a2[:, h:h + 1] * oi_ref[2, h]
        a_ref[:, hs] = o.astype(BF16)
    res = x_ref[...] + jnp.dot(a_ref[...], w_ref[...], preferred_element_type=F32)
    out_ref[...] = res
    out2_ref[...] = res


def _dilated_combine(os, ls, w_out, x, *, tm=512):
    b, s, d = x.shape
    hw = w_out.shape[0]
    tm = _pick(s, tm)
    per_b = s // tm

    def dspec(a):
        r, w = a.shape[1], a.shape[3]
        assert tm % (r * BF16_SUBLANES) == 0
        return pl.BlockSpec((None, r, tm // r, w), lambda i: (i // per_b, 0, i % per_b, 0))

    xspec = pl.BlockSpec((None, tm, d), lambda i: (i // per_b, i % per_b, 0))
    return pl.pallas_call(
        functools.partial(_dilated_combine_kernel, heads=hw // HEAD_DIM),
        grid=(b * per_b,),
        in_specs=[dspec(a) for a in os] + [dspec(a) for a in ls] + [pl.BlockSpec((hw, d), lambda i: (0, 0)), xspec],
        out_specs=[xspec, xspec],
        out_shape=[jax.ShapeDtypeStruct((b, s, d), F32)] * 2,
        scratch_shapes=[pltpu.VMEM((tm, hw), BF16), pltpu.VMEM((3, hw // HEAD_DIM, tm, HEAD_DIM), F32),
                        pltpu.VMEM((3, tm, LANES), F32)],
        compiler_params=_params(("parallel",)),
        name="dilated_combine",
    )(*os, *ls, w_out, x)


def _dilated_layer(x, g_norm, w_qkv, w_out, t5_bias):
    hw = w_out.shape[0]
    heads = hw // HEAD_DIM
    assert len(B_PAIRS) == 3 and w_qkv.shape[1] == 9 * hw
    os, ls = [], []
    for g, (window, dilation) in enumerate(B_PAIRS):
        assert window // (2 * dilation) == DIL_HALF
        qkv = _norm_matmul(x, g_norm, w_qkv, col0=g * 3 * hw, n=3 * hw, dilation=dilation)
        bias = _dilated_bias(t5_bias[:, g * heads:(g + 1) * heads], dilation)
        o, l = _dilated_attn(qkv, bias, dilation, heads)
        os.append(o)
        ls.append(l)
    return _dilated_combine(os, ls, w_out, x)


NA_QROWS = 8
NA_KBLK = 4
NA_WIN = NA_QROWS + NA_ROWS
NA_HEAD_GROUP = 4


def _na_bias_table(rpb):
    scale = HEAD_DIM ** -0.5
    qc = np.arange(GRID_W)[None, :]
    kc = np.arange(GRID_W)[:, None]
    cstart = np.clip(qc - NA_COLS // 2, 0, GRID_W - NA_COLS)
    ok = (kc >= cstart) & (kc < cstart + NA_COLS)
    dcol = np.clip(kc - qc, -(NA_COLS - 1), NA_COLS - 1) + NA_COLS - 1
    tab = _lookup_rows(jnp.moveaxis(rpb.astype(F32), 2, 0), dcol)
    tab = jnp.moveaxis(tab, (2, 3), (0, 1))
    tab = jnp.where(ok[None, None], tab, NEG_INF) / scale
    hi = tab.astype(BF16)
    lo = (tab - hi.astype(F32)).astype(BF16)
    out = jnp.concatenate([hi, lo], axis=-1)
    return out.reshape(rpb.shape[0], (2 * NA_ROWS - 1) * GRID_W, 2 * GRID_W)


def _na_attn_kernel(bk_ref, q_ref, k0_ref, k1_ref, k2_ref, k3_ref, v0_ref, v1_ref, v2_ref, v3_ref,
                    o_ref, kw_ref, vw_ref, bias_ref, *, heads, rows):
    rb = pl.program_id(1)
    blk = NA_KBLK * GRID_W
    for j, (kr, vr) in enumerate(((k0_ref, v0_ref), (k1_ref, v1_ref), (k2_ref, v2_ref), (k3_ref, v3_ref))):
        kw_ref[j * blk:(j + 1) * blk, :] = kr[...]
        vw_ref[j * blk:(j + 1) * blk, :] = vr[...]
    ws = NA_KBLK * jnp.clip(2 * rb - 1, 0, rows // NA_KBLK - NA_WIN // NA_KBLK)
    nkeys = NA_ROWS * GRID_W
    eye = (lax.broadcasted_iota(jnp.int32, (GRID_W, 2 * GRID_W), 0)
           == lax.broadcasted_iota(jnp.int32, (GRID_W, 2 * GRID_W), 1) % GRID_W)
    eye2 = jnp.where(eye, 1.0, 0.0).astype(BF16)
    scale = HEAD_DIM ** -0.5
    nt = (((1,), (1,)), ((), ()))

    def bias_of(h, bsl):
        return lax.dot_general(eye2, bk_ref[h, bsl, :], nt, preferred_element_type=F32)

    def row_loop(bias_is_staged):
        def row_body(rr, carry):
            r = rb * NA_QROWS + rr
            rs = jnp.clip(r - NA_ROWS // 2, 0, rows - NA_ROWS)
            ksl = pl.ds(pl.multiple_of((rs - ws) * GRID_W, GRID_W), nkeys)
            bsl = pl.ds(pl.multiple_of((rs - r + NA_ROWS - 1) * GRID_W, GRID_W), nkeys)
            qsl = pl.ds(pl.multiple_of(rr * GRID_W, GRID_W), GRID_W)
            for h0 in range(0, heads, NA_HEAD_GROUP):
                hss = [slice(h * HEAD_DIM, (h + 1) * HEAD_DIM) for h in range(h0, h0 + NA_HEAD_GROUP)]
                ss = [lax.dot_general(q_ref[qsl, hs], kw_ref[ksl, hs], nt, preferred_element_type=F32)
                      + (bias_ref[h0 + j] if bias_is_staged else bias_of(h0 + j, bsl))
                      for j, hs in enumerate(hss)]
                ps = []
                for s in ss:
                    s = s * scale
                    p = jnp.exp(s - jnp.max(s, axis=-1, keepdims=True))
                    ps.append((p * (1.0 / jnp.sum(p, axis=-1, keepdims=True))).astype(BF16))
                for p, hs in zip(ps, hss):
                    o_ref[qsl, hs] = jnp.dot(p, vw_ref[ksl, hs], preferred_element_type=F32).astype(o_ref.dtype)
            return carry
        lax.fori_loop(0, NA_QROWS, row_body, 0)

    interior = (rb > 0) & (rb < pl.num_programs(1) - 1)

    @pl.when(interior)
    def _():
        centre = pl.ds((NA_ROWS - 1 - NA_ROWS // 2) * GRID_W, nkeys)
        for h in range(heads):
            bias_ref[h] = bias_of(h, centre)
        row_loop(True)

    @pl.when(jnp.logical_not(interior))
    def _():
        row_loop(False)


def _na_attn(qkv, bk, heads):
    b, s, _ = qkv.shape
    hw = heads * HEAD_DIM
    rows = s // GRID_W
    assert s % GRID_W == 0 and rows % NA_QROWS == 0 and rows >= NA_WIN and 2 * GRID_W == LANES
    assert heads % NA_HEAD_GROUP == 0
    nwb = rows // NA_KBLK - NA_WIN // NA_KBLK
    blk = NA_KBLK * GRID_W

    def kv_spec(c, j):
        return pl.BlockSpec((None, blk, hw), lambda bi, rb: (bi, jnp.clip(2 * rb - 1, 0, nwb) + j, c))

    o = pl.pallas_call(
        functools.partial(_na_attn_kernel, heads=heads, rows=rows),
        grid=(b, rows // NA_QROWS),
        in_specs=[pl.BlockSpec(bk.shape, lambda bi, rb: (0, 0, 0)),
                  pl.BlockSpec((None, NA_QROWS * GRID_W, hw), lambda bi, rb: (bi, rb, 0))]
                 + [kv_spec(1, j) for j in range(4)] + [kv_spec(2, j) for j in range(4)],
        out_specs=pl.BlockSpec((None, NA_QROWS * GRID_W, hw), lambda bi, rb: (bi, rb, 0)),
        out_shape=jax.ShapeDtypeStruct((b, s, hw), BF16),
        scratch_shapes=[pltpu.VMEM((NA_WIN * GRID_W, hw), BF16), pltpu.VMEM((NA_WIN * GRID_W, hw), BF16),
                        pltpu.VMEM((heads, GRID_W, NA_ROWS * GRID_W), F32)],
        compiler_params=_params(("parallel", "parallel")),
        name="na_attn",
    )(bk, *([qkv] * 9))
    return o.reshape(b * s, hw)


def _neighbourhood_layer(x, g_norm, w_qkv, w_out, rpb):
    b, s, d = x.shape
    hw = w_out.shape[0]
    assert w_qkv.shape[1] == 3 * hw
    qkv = _norm_matmul(x, g_norm, w_qkv).reshape(b, s, 3 * hw)
    o = _na_attn(qkv, _na_bias_table(rpb), hw // HEAD_DIM)
    return [y.reshape(b, s, d) for y in _matmul_res(o, w_out, x.reshape(b * s, d))]


def _router_kernel(x_ref, g_ref, wt_ref, aff_ref):
    hn = _rmsnorm_rows(x_ref[...], g_ref[...]).astype(BF16)
    logits = lax.dot_general(wt_ref[...], hn, (((1,), (1,)), ((), ())), preferred_element_type=F32)
    m = jnp.max(logits, axis=0, keepdims=True)
    p = jnp.exp(logits - m)
    aff_ref[...] = p / jnp.sum(p, axis=0, keepdims=True)


def _router(x, g, w_router_t, *, tm=512):
    t, d = x.shape
    e = w_router_t.shape[0]
    tm = _pick(t, tm)
    return pl.pallas_call(
        _router_kernel,
        grid=(t // tm,),
        in_specs=[pl.BlockSpec((tm, d), lambda i: (i, 0)),
                  pl.BlockSpec((1, d), lambda i: (0, 0)),
                  pl.BlockSpec((e, d), lambda i: (0, 0))],
        out_specs=pl.BlockSpec((e, tm), lambda i: (0, i)),
        out_shape=jax.ShapeDtypeStruct((e, t), F32),
        compiler_params=_params(("parallel",)),
        name="moe_router",
    )(x, g.reshape(1, d), w_router_t)


def _topk_mask_kernel(aff_ref, sel_ref, *, cap):
    bits = pltpu.bitcast(aff_ref[...], jnp.int32)
    e, t = bits.shape

    def count(mask):
        return jnp.sum(jnp.where(mask, 1, 0), axis=1, keepdims=True)

    def value_step(_, c):
        lo, hi = c
        mid = lo + (hi - lo) // 2
        ok = count(bits >= mid) >= cap
        return jnp.where(ok, mid, lo), jnp.where(ok, hi, mid)

    lo0 = jnp.zeros((e, 1), jnp.int32)
    hi0 = jnp.full((e, 1), 0x7F800000, jnp.int32)
    thr, _ = lax.fori_loop(0, 31, value_step, (lo0, hi0))
    above = bits > thr
    tie = bits == thr
    need = cap - count(above)
    tok = lax.broadcasted_iota(jnp.int32, (e, t), 1)

    def index_step(_, c):
        lo, hi = c
        mid = lo + (hi - lo) // 2
        ok = count(tie & (tok < mid)) >= need
        return jnp.where(ok, lo, mid), jnp.where(ok, mid, hi)

    _, bound = lax.fori_loop(0, max(1, math.ceil(math.log2(t))), index_step,
                             (jnp.zeros((e, 1), jnp.int32), jnp.full((e, 1), t, jnp.int32)))
    sel_ref[...] = jnp.where(above | (tie & (tok < bound)), 1, 0)


def _topk_mask(aff_t, cap):
    e, t = aff_t.shape
    return pl.pallas_call(
        functools.partial(_topk_mask_kernel, cap=cap),
        grid=(1,),
        in_specs=[pl.BlockSpec((e, t), lambda i: (0, 0))],
        out_specs=pl.BlockSpec((e, t), lambda i: (0, 0)),
        out_shape=jax.ShapeDtypeStruct((e, t), jnp.int32),
        compiler_params=_params(("arbitrary",)),
        name="moe_topk_mask",
    )(aff_t)


def _compact_kernel(sel_ref, aff_ref, idx_ref, gate_ref, cnt_ref, bound_ref, piece_ref, *, tj):
    nb = sel_ref.shape[0]

    @pl.when(pl.program_id(1) == 0)
    def _():
        m = sel_ref[...].astype(F32).astype(BF16)
        tri = jnp.where(lax.broadcasted_iota(jnp.int32, (LANES, LANES), 0)
                        <= lax.broadcasted_iota(jnp.int32, (LANES, LANES), 1), 1.0, 0.0).astype(BF16)
        cnt_ref[...] = jnp.dot(m, tri, preferred_element_type=F32).astype(BF16)
        tot = lax.dot_general(jnp.ones((8, LANES), BF16), m, (((1,), (1,)), ((), ())),
                              preferred_element_type=F32)
        trib = jnp.where(lax.broadcasted_iota(jnp.int32, (nb, nb), 0)
                         <= lax.broadcasted_iota(jnp.int32, (nb, nb), 1), 1.0, 0.0).astype(BF16)
        incl = jnp.dot(tot.astype(BF16), trib, preferred_element_type=F32)
        bound_ref[0:8, :] = incl
        bound_ref[8:16, :] = incl - tot
        bits = pltpu.bitcast(aff_ref[...], jnp.int32)
        for k in range(4):
            piece_ref[k] = ((bits >> (8 * k)) & 0xFF).astype(F32).astype(BF16)

    jf = (pl.program_id(1) * tj + lax.broadcasted_iota(jnp.int32, (tj, 1), 0)).astype(F32)
    blk = jnp.sum(jnp.where(bound_ref[0:1, :] <= jf, 1, 0), axis=1, keepdims=True)
    onehot = lax.broadcasted_iota(jnp.int32, (tj, nb), 1) == blk
    start = jnp.sum(jnp.where(onehot, bound_ref[8:9, :], 0.0), axis=1, keepdims=True)
    oh = jnp.where(onehot, 1.0, 0.0).astype(BF16)
    cnt = jnp.dot(oh, cnt_ref[...], preferred_element_type=F32)
    lane = jnp.sum(jnp.where(cnt < jf - start + 1.0, 1, 0), axis=1, keepdims=True)
    idx_ref[...] = blk * LANES + lane
    pick = lax.broadcasted_iota(jnp.int32, (tj, LANES), 1) == lane
    bits = jnp.zeros((tj, 1), jnp.int32)
    for k in range(4):
        byte = jnp.dot(oh, piece_ref[k], preferred_element_type=F32)
        bits = bits | (jnp.sum(jnp.where(pick, byte, 0.0), axis=1, keepdims=True).astype(jnp.int32) << (8 * k))
    gate_ref[...] = pltpu.bitcast(bits, F32)


def _compact(sel, aff_t, cap, *, tj=512):
    e, t = sel.shape
    assert t % LANES == 0
    nb = t // LANES
    tj = _pick(cap, tj)
    nj = cap // tj
    blocks = pl.BlockSpec((None, nb, LANES), lambda ei, j: (ei, 0, 0))
    col = pl.BlockSpec((tj, 1), lambda ei, j: (ei * nj + j, 0))
    return pl.pallas_call(
        functools.partial(_compact_kernel, tj=tj),
        grid=(e, nj),
        in_specs=[blocks, blocks],
        out_specs=[col, col],
        out_shape=[jax.ShapeDtypeStruct((e * cap, 1), jnp.int32), jax.ShapeDtypeStruct((e * cap, 1), F32)],
        scratch_shapes=[pltpu.VMEM((nb, LANES), BF16), pltpu.VMEM((16, nb), F32), pltpu.VMEM((4, nb, LANES), BF16)],
        compiler_params=_params(("parallel", "arbitrary")),
        name="moe_compact",
    )(sel.reshape(e, nb, LANES), aff_t.reshape(e, nb, LANES))


X_SEM, OUT_SEM, SCATTER_SEM = 0, 1, 2


def _moe_ffn_kernel(idx_ref, gate_ref, gn_ref, wg_ref, wu_ref, wd_ref, x_hbm, yin_hbm, out_hbm,
                    xg_ref, ob_ref, h_ref, acc_ref, sem, *, tm, nt, nf, n_tiles):
    del yin_hbm
    i = pl.program_id(1)
    tile = pl.program_id(0) * nt + i
    base = tile * tm
    f = pl.program_id(2)
    slot = tile % 2
    first_of_expert = i == 0
    after_first = i == 1 % nt

    def row_copy(hbm, buf, first, r, s, to_vmem):
        tok = idx_ref[first + r]
        src, dst = hbm.at[pl.ds(tok, 1), :], buf.at[pl.ds(r, 1), :]
        return (pltpu.make_async_copy(src, dst, sem.at[s]) if to_vmem
                else pltpu.make_async_copy(dst, src, sem.at[s]))

    def start_rows(hbm, buf, first, s, to_vmem):
        def body(r, c):
            row_copy(hbm, buf, first, r, s, to_vmem).start()
            return c
        lax.fori_loop(0, tm, body, 0, unroll=32)

    def wait_rows(hbm, buf, s, to_vmem):
        whole = hbm.at[pl.ds(0, tm), :]
        (pltpu.make_async_copy(whole, buf, sem.at[s]) if to_vmem
         else pltpu.make_async_copy(buf, whole, sem.at[s])).wait()

    def wait_scatter(s):
        wait_rows(out_hbm, ob_ref.at[s], SCATTER_SEM + s, False)

    @pl.when(f == 0)
    def _():
        @pl.when(tile == 0)
        def _():
            start_rows(x_hbm, xg_ref, base, X_SEM, True)

        @pl.when((tile >= 2) & jnp.logical_not(after_first))
        def _():
            wait_scatter(slot)

        @pl.when((tile >= 1) & first_of_expert)
        def _():
            wait_scatter(1 - slot)

        wait_rows(x_hbm, xg_ref, X_SEM, True)
        h_ref[...] = _rmsnorm_rows(xg_ref[...], gn_ref[...]).astype(BF16)
        acc_ref[...] = jnp.zeros_like(acc_ref)

    nxt = jnp.minimum(tile + 1, n_tiles - 1) * tm
    per_step = tm // nf
    for r in range(per_step):
        row = f * per_step + r
        row_copy(out_hbm, ob_ref.at[slot], base, row, OUT_SEM, True).start()
        row_copy(x_hbm, xg_ref, nxt, row, X_SEM, True).start()

    h = h_ref[...]
    g = jnp.dot(h, wg_ref[...].astype(BF16), preferred_element_type=F32)
    u = jnp.dot(h, wu_ref[...].astype(BF16), preferred_element_type=F32)
    hid = (g * (1.0 / (1.0 + jnp.exp(-g))) * u).astype(BF16)
    acc_ref[...] += jnp.dot(hid, wd_ref[...].astype(BF16), preferred_element_type=F32)

    @pl.when(f == nf - 1)
    def _():
        wait_rows(out_hbm, ob_ref.at[slot], OUT_SEM, True)
        ob_ref[slot] = ob_ref[slot] + acc_ref[...] * gate_ref[...]
        start_rows(out_hbm, ob_ref.at[slot], base, SCATTER_SEM + slot, False)

        @pl.when(tile == n_tiles - 1)
        def _():
            @pl.when((tile >= 1) & jnp.logical_not(first_of_expert))
            def _():
                wait_scatter(1 - slot)

            wait_scatter(slot)
            wait_rows(x_hbm, xg_ref, X_SEM, True)


def _moe_ffn(x, x_copy, idx, gate, n_experts, g_norm, wg, wu, wd, layer, *, tm=1024, tf=256):
    t, d = x.shape
    e = n_experts
    cap = idx.shape[0] // e
    fdim = wg.shape[3]
    tm = _pick(cap, tm)
    tf = _pick(fdim, tf)
    nt, nf = cap // tm, fdim // tf
    assert tm % nf == 0
    grid_spec = pltpu.PrefetchScalarGridSpec(
        num_scalar_prefetch=1,
        grid=(e, nt, nf),
        in_specs=[pl.BlockSpec((tm, 1), lambda ei, i, f, idx: (ei * nt + i, 0)),
                  pl.BlockSpec((1, d), lambda ei, i, f, idx: (0, 0)),
                  pl.BlockSpec((None, None, d, tf), lambda ei, i, f, idx: (layer, ei, 0, f)),
                  pl.BlockSpec((None, None, d, tf), lambda ei, i, f, idx: (layer, ei, 0, f)),
                  pl.BlockSpec((None, None, tf, d), lambda ei, i, f, idx: (layer, ei, f, 0)),
                  pl.BlockSpec(memory_space=pl.ANY),
                  pl.BlockSpec(memory_space=pl.ANY)],
        out_specs=pl.BlockSpec(memory_space=pl.ANY),
        scratch_shapes=[pltpu.VMEM((tm, d), F32), pltpu.VMEM((2, tm, d), F32), pltpu.VMEM((tm, d), BF16),
                        pltpu.VMEM((tm, d), F32), pltpu.SemaphoreType.DMA((4,))],
    )
    return pl.pallas_call(
        functools.partial(_moe_ffn_kernel, tm=tm, nt=nt, nf=nf, n_tiles=e * nt),
        grid_spec=grid_spec,
        out_shape=jax.ShapeDtypeStruct((t, d), F32),
        input_output_aliases={7: 0},
        compiler_params=_params(("arbitrary", "arbitrary", "arbitrary")),
        name="moe_ffn",
    )(idx.reshape(e * cap), gate, g_norm.reshape(1, d), wg, wu, wd, x, x_copy)


def _moe_layer(x, x_copy, g_norm, w_router, wg, wu, wd, layer):
    b, s, d = x.shape
    t = b * s
    e = w_router.shape[1]
    cap = EC_CAPACITY * t // e
    xt = x.reshape(t, d)
    aff_t = _router(xt, g_norm, w_router.T.astype(BF16))
    sel = _topk_mask(aff_t, cap)
    idx, gate = _compact(sel, aff_t, cap)
    return _moe_ffn(xt, x_copy.reshape(t, d), idx, gate, e, g_norm, wg, wu, wd, layer).reshape(b, s, d)


def _trunk(x, w):
    for i in range(w["norm_mix"].shape[0]):
        kind, j = i % N_MIXERS, i // N_MIXERS
        if kind == 0:
            x, x2 = _fourier_layer(x, w["norm_mix"][i], w["a_w_in"][j], w["a_w_out"][j])
        elif kind == 1:
            x, x2 = _dilated_layer(x, w["norm_mix"][i], w["b_w_qkv"][j], w["b_w_out"][j], w["t5_bias"])
        else:
            x, x2 = _neighbourhood_layer(x, w["norm_mix"][i], w["c_w_qkv"][j], w["c_w_out"][j], w["c_rpb"][j])
        x = _moe_layer(x, x2, w["norm_ffn"][i], w["moe_router"][i], w["moe_w_gate"], w["moe_w_up"],
                       w["moe_w_down"], i)
    b, s, d = x.shape
    return _rmsnorm(x.reshape(b * s, d), w["norm_final"]).reshape(b, s, d)


def kernel(x_prompt, x_sample, norm_mix, norm_ffn, norm_final, a_w_in, a_w_out, b_w_qkv, b_w_out, t5_bias,
           c_w_qkv, c_w_out, c_rpb, moe_router, moe_w_gate, moe_w_up, moe_w_down):
    bf = lambda a: a.astype(BF16)
    w = dict(norm_mix=norm_mix, norm_ffn=norm_ffn, norm_final=norm_final, a_w_in=bf(a_w_in), a_w_out=bf(a_w_out),
             b_w_qkv=bf(b_w_qkv), b_w_out=bf(b_w_out), t5_bias=t5_bias, c_w_qkv=bf(c_w_qkv), c_w_out=bf(c_w_out),
             c_rpb=c_rpb, moe_router=moe_router, moe_w_gate=moe_w_gate, moe_w_up=moe_w_up, moe_w_down=moe_w_down)
    return _trunk(x_prompt, w), _trunk(x_sample, w)
```

```python
import functools
import math

import numpy as np
import jax
import jax.numpy as jnp
from jax import lax
from jax.experimental import pallas as pl
from jax.experimental.pallas import tpu as pltpu

F32 = jnp.float32
BF16 = jnp.bfloat16

RMS_EPS = 1e-6
NEG_INF = -1e30
HEAD_DIM = 128
GRID_W = 64
A_GROUPS = 4
B_PAIRS = ((128, 1), (512, 4), (2048, 16))
T5_BUCKETS = 32
T5_MAX_DIST = 1024
NA_ROWS = 8
NA_COLS = 16
EC_CAPACITY = 2
N_MIXERS = 3

V7X_VMEM_BYTES = 64 * 1024 * 1024
VMEM_LIMIT = V7X_VMEM_BYTES - 8 * 1024 * 1024
LANES = 128
BF16_SUBLANES = 16


def _params(sem):
    return pltpu.CompilerParams(dimension_semantics=sem, vmem_limit_bytes=VMEM_LIMIT)


def _pick(n, pref):
    t = min(n, pref)
    while n % t:
        t //= 2
    return t


def _rmsnorm_rows(x, g):
    ms = jnp.mean(x * x, axis=-1, keepdims=True)
    return (x * lax.rsqrt(ms + RMS_EPS)) * g


def _norm_matmul_kernel(x_ref, g_ref, w_ref, o_ref, h_ref, r_ref, *, rows, dilation):
    @pl.when(pl.program_id(1) == 0)
    def _():
        def body(c, carry):
            sl = pl.ds(pl.multiple_of(c * rows, rows), rows)
            h_ref[sl, :] = _rmsnorm_rows(x_ref[sl, :], g_ref[...]).astype(BF16)
            return carry
        lax.fori_loop(0, x_ref.shape[0] // rows, body, 0)

    res = jnp.dot(h_ref[...], w_ref[...], preferred_element_type=F32)
    if dilation == 1:
        o_ref[0] = res.astype(o_ref.dtype)
    else:
        chunks = [slice(c * LANES, (c + 1) * LANES) for c in range(r_ref.shape[0])]
        for c, cs in enumerate(chunks):
            r_ref[c] = res[:, cs]
        per = r_ref.shape[1] // dilation
        for rho in range(dilation):
            for c, cs in enumerate(chunks):
                o_ref[rho, :, cs] = r_ref[c, pl.ds(rho, per, stride=dilation), :].astype(o_ref.dtype)


def _norm_matmul(x, g, w, *, col0=0, n=None, dilation=1, tm=1024, tn=1024):
    b, s, d = x.shape
    n = w.shape[1] if n is None else n
    tm = _pick(s, tm)
    tn = _pick(n, tn)
    assert col0 % tn == 0 and tm % (dilation * BF16_SUBLANES) == 0
    rows = _pick(tm, 256)
    per_b = s // tm
    cb0 = col0 // tn
    out = pl.pallas_call(
        functools.partial(_norm_matmul_kernel, rows=rows, dilation=dilation),
        grid=(b * per_b, n // tn),
        in_specs=[pl.BlockSpec((None, tm, d), lambda i, j: (i // per_b, i % per_b, 0)),
                  pl.BlockSpec((1, d), lambda i, j: (0, 0)),
                  pl.BlockSpec((d, tn), lambda i, j: (0, cb0 + j))],
        out_specs=pl.BlockSpec((None, dilation, tm // dilation, tn), lambda i, j: (i // per_b, 0, i % per_b, j)),
        out_shape=jax.ShapeDtypeStruct((b, dilation, s // dilation, n), BF16),
        scratch_shapes=[pltpu.VMEM((tm, d), BF16),
                        pltpu.VMEM((tn // LANES, tm, LANES) if dilation > 1 else (1, 8, LANES), F32)],
        compiler_params=_params(("parallel", "arbitrary")),
        name="norm_matmul",
    )(x, g.reshape(1, d), w)
    return out


def _matmul_res_kernel(a_ref, w_ref, x_ref, o_ref, o2_ref):
    res = x_ref[...] + jnp.dot(a_ref[...], w_ref[...], preferred_element_type=F32)
    o_ref[...] = res
    o2_ref[...] = res


def _matmul_res(a, w, x, *, tm=512):
    t, k = a.shape
    d = w.shape[1]
    tm = _pick(t, tm)
    row = pl.BlockSpec((tm, d), lambda i: (i, 0))
    return pl.pallas_call(
        _matmul_res_kernel,
        grid=(t // tm,),
        in_specs=[pl.BlockSpec((tm, k), lambda i: (i, 0)), pl.BlockSpec((k, d), lambda i: (0, 0)), row],
        out_specs=[row, row],
        out_shape=[jax.ShapeDtypeStruct((t, d), F32)] * 2,
        compiler_params=_params(("parallel",)),
        name="matmul_res",
    )(a, w, x)


def _rmsnorm_kernel(x_ref, g_ref, o_ref):
    o_ref[...] = _rmsnorm_rows(x_ref[...], g_ref[...])


def _rmsnorm(x, g, *, tm=256):
    t, d = x.shape
    tm = _pick(t, tm)
    return pl.pallas_call(
        _rmsnorm_kernel,
        grid=(t // tm,),
        in_specs=[pl.BlockSpec((tm, d), lambda i: (i, 0)),
                  pl.BlockSpec((1, d), lambda i: (0, 0))],
        out_specs=pl.BlockSpec((tm, d), lambda i: (i, 0)),
        out_shape=jax.ShapeDtypeStruct((t, d), F32),
        compiler_params=_params(("parallel",)),
        name="final_rmsnorm",
    )(x, g.reshape(1, d))


FFT_S2 = 128
FFT_STAGE2_COLS = 1024


def _dft_tables(seq, ch):
    s1n = seq // FFT_S2
    k1 = np.arange(s1n, dtype=np.float64)
    ang1 = 2.0 * np.pi * np.outer(k1, k1) / s1n
    m1 = np.concatenate([np.cos(ang1), -np.sin(ang1)], axis=0)
    s2 = np.arange(FFT_S2, dtype=np.float64)
    kk = k1[:, None, None] + s1n * s2[None, :, None]
    ang2 = 2.0 * np.pi * ((kk * s2[None, None, :]) % seq) / seq
    gc, gs = np.cos(ang2), np.sin(ang2)
    g = np.concatenate([np.concatenate([gc, gs], axis=2),
                        np.concatenate([-gs, gc], axis=2)], axis=1)
    c = np.arange(ch, dtype=np.float64)
    angc = 2.0 * np.pi * (np.outer(c, c) % ch) / ch
    cs = np.concatenate([np.cos(angc), np.sin(angc)], axis=0) / math.sqrt(seq * ch)
    return (jnp.asarray(m1, F32).astype(BF16), jnp.asarray(g, F32).astype(BF16),
            jnp.asarray(cs, F32).astype(BF16))


def _fft_stage1_kernel(m_ref, u_ref, y_ref, uf_ref, yf_ref):
    s1n, sb, tn = u_ref.shape
    chunks = [slice(c * LANES, (c + 1) * LANES) for c in range(tn // LANES)]
    for c, cs in enumerate(chunks):
        uf_ref[c] = u_ref[:, :, cs].astype(F32).reshape(s1n * sb, LANES)
    for j in range(sb):
        rows = jnp.concatenate([uf_ref[c, pl.ds(j, s1n, stride=sb), :] for c in range(len(chunks))], axis=1)
        res = jnp.dot(m_ref[...], rows.astype(BF16), preferred_element_type=F32)
        for c, cs in enumerate(chunks):
            yf_ref[c, pl.ds(j, 2 * s1n, stride=sb), :] = res[:, cs]
    for c, cs in enumerate(chunks):
        y_ref[:, :, cs] = yf_ref[c].reshape(2 * s1n, sb, LANES).astype(y_ref.dtype)


def _fft_stage1(u, m1, *, tn=512):
    b, s1n, _, d = u.shape
    tn = _pick(d, tn)
    sb = BF16_SUBLANES
    return pl.pallas_call(
        _fft_stage1_kernel,
        grid=(b, FFT_S2 // sb, d // tn),
        in_specs=[pl.BlockSpec((2 * s1n, s1n), lambda i, j, c: (0, 0)),
                  pl.BlockSpec((None, s1n, sb, tn), lambda i, j, c: (i, 0, j, c))],
        out_specs=pl.BlockSpec((None, 2 * s1n, sb, tn), lambda i, j, c: (i, 0, j, c)),
        out_shape=jax.ShapeDtypeStruct((b, 2 * s1n, FFT_S2, d), BF16),
        scratch_shapes=[pltpu.VMEM((tn // LANES, s1n * sb, LANES), F32),
                        pltpu.VMEM((tn // LANES, 2 * s1n * sb, LANES), F32)],
        compiler_params=_params(("parallel", "parallel", "parallel")),
        name="fft_stage1",
    )(m1, u)


def _fft_stage2_kernel(g_ref, yr_ref, yi_ref, cs_ref, f_ref, ff_ref):
    kb = g_ref.shape[0]
    ch = cs_ref.shape[1]
    for j in range(kb):
        y = jnp.concatenate([yr_ref[j], yi_ref[j]], axis=0)
        xc = jnp.dot(g_ref[j], y, preferred_element_type=F32).astype(BF16)
        for q in range(y.shape[1] // ch):
            cols = slice(q * ch, (q + 1) * ch)
            xg = jnp.concatenate([xc[:FFT_S2, cols], xc[FFT_S2:, cols]], axis=1)
            ff_ref[:, j, cols] = jnp.dot(xg, cs_ref[...], preferred_element_type=F32)
    f_ref[...] = ff_ref[...].astype(f_ref.dtype)


def _fft_stage2(y, g, cs):
    b, s1n2, _, d = y.shape
    s1n = s1n2 // 2
    ch = cs.shape[1]
    kb = BF16_SUBLANES
    assert s1n % kb == 0
    nk = s1n // kb
    tc = _pick(d, FFT_STAGE2_COLS)
    assert tc % ch == 0
    return pl.pallas_call(
        _fft_stage2_kernel,
        grid=(b, nk, d // tc),
        in_specs=[pl.BlockSpec((kb, 2 * FFT_S2, 2 * FFT_S2), lambda i, j, q: (j, 0, 0)),
                  pl.BlockSpec((None, kb, FFT_S2, tc), lambda i, j, q: (i, j, 0, q)),
                  pl.BlockSpec((None, kb, FFT_S2, tc), lambda i, j, q: (i, j + nk, 0, q)),
                  pl.BlockSpec(cs.shape, lambda i, j, q: (0, 0))],
        out_specs=pl.BlockSpec((None, FFT_S2, kb, tc), lambda i, j, q: (i, 0, j, q)),
        out_shape=jax.ShapeDtypeStruct((b, FFT_S2, s1n, d), BF16),
        scratch_shapes=[pltpu.VMEM((FFT_S2, kb, tc), F32)],
        compiler_params=_params(("parallel", "parallel", "parallel")),
        name="fft_stage2",
    )(g, y, y, cs)


def _fourier_layer(x, g_norm, w_in, w_out):
    b, s, d = x.shape
    assert s % FFT_S2 == 0 and d % A_GROUPS == 0
    s1n = s // FFT_S2
    m1, g, cs = _dft_tables(s, d // A_GROUPS)
    u = _norm_matmul(x, g_norm, w_in)
    y = _fft_stage1(u.reshape(b, s1n, FFT_S2, d), m1)
    f = _fft_stage2(y, g, cs)
    return [o.reshape(b, s, d) for o in _matmul_res(f.reshape(b * s, d), w_out, x.reshape(b * s, d))]


DIL_TQ = 128
DIL_HALF = 64
DIL_HEAD_GROUP = 4
DIL_QBLOCKS = 4


def _lookup_rows(table, index):
    onehot = jnp.asarray(np.eye(table.shape[0], dtype=np.float32)[index])
    return jnp.tensordot(onehot, table.astype(F32), axes=1, precision=lax.Precision.HIGHEST)


def _t5_bucket_np(rel):
    nb = T5_BUCKETS // 2
    max_exact = nb // 2
    n = np.abs(rel)
    large = max_exact + (np.log(np.maximum(n, 1) / max_exact) / math.log(T5_MAX_DIST / max_exact)
                         * (nb - max_exact)).astype(np.int32)
    large = np.minimum(large, nb - 1)
    return (np.where(rel > 0, nb, 0) + np.where(n < max_exact, n, large)).astype(np.int32)


def _dilated_bias(t5_bias_g, dilation):
    off = np.arange(DIL_TQ + 2 * DIL_HALF)[None, :] - DIL_HALF - np.arange(DIL_TQ)[:, None]
    band = np.abs(off) <= DIL_HALF
    bias = _lookup_rows(t5_bias_g, _t5_bucket_np(off * dilation)).transpose(2, 0, 1)
    return jnp.where(band[None], bias.astype(F32), NEG_INF)


def _dilated_attn_kernel(bias_ref, q_ref, *refs, heads, sub_len, qb):
    k_refs, v_refs = refs[:qb + 2], refs[qb + 2:2 * qb + 4]
    o_ref, l_ref = refs[2 * qb + 4:]
    i = pl.program_id(2)
    nk = DIL_TQ + 2 * DIL_HALF
    lane = lax.broadcasted_iota(jnp.int32, (DIL_TQ, LANES), 1)
    scale = HEAD_DIM ** -0.5

    def window(blocks, jb, hs):
        return jnp.concatenate([blocks[jb][DIL_TQ - DIL_HALF:, hs], blocks[jb + 1][:, hs],
                                blocks[jb + 2][:DIL_HALF, hs]], axis=0)

    for jb in range(qb):
        rows = slice(jb * DIL_TQ, (jb + 1) * DIL_TQ)
        kpos = (i * qb + jb) * DIL_TQ - DIL_HALF + lax.broadcasted_iota(jnp.int32, (1, nk), 1)
        valid = (kpos >= 0) & (kpos < sub_len)
        lse_all = jnp.zeros((DIL_TQ, LANES), F32)
        for h0 in range(0, heads, DIL_HEAD_GROUP):
            hss = [slice(h * HEAD_DIM, (h + 1) * HEAD_DIM) for h in range(h0, h0 + DIL_HEAD_GROUP)]
            ss = [lax.dot_general(q_ref[rows, hs], window(k_refs, jb, hs), (((1,), (1,)), ((), ())),
                                  preferred_element_type=F32) for hs in hss]
            ps = []
            for j, s in enumerate(ss):
                s = jnp.where(valid, s * scale + bias_ref[h0 + j], NEG_INF)
                m = jnp.max(s, axis=-1, keepdims=True)
                p = jnp.exp(s - m)
                den = jnp.sum(p, axis=-1, keepdims=True)
                ps.append((p * (1.0 / den)).astype(BF16))
                lse_all = jnp.where(lane == h0 + j, m + jnp.log(den), lse_all)
            for p, hs in zip(ps, hss):
                o = jnp.dot(p, window(v_refs, jb, hs), preferred_element_type=F32)
                o_ref[rows, hs] = o.astype(o_ref.dtype)
        l_ref[rows, :] = lse_all


def _dilated_attn(qkv, bias, dilation, heads):
    b, r, sub_len, _ = qkv.shape
    hw = heads * HEAD_DIM
    assert r == dilation and sub_len % DIL_TQ == 0
    nb = sub_len // DIL_TQ
    qb = _pick(nb, DIL_QBLOCKS)
    rows = qb * DIL_TQ

    def kv_spec(c, j):
        return pl.BlockSpec((None, None, DIL_TQ, hw),
                            lambda bi, rho, i: (bi, rho, jnp.clip(i * qb + j - 1, 0, nb - 1), c))

    def row_spec(w, c):
        return pl.BlockSpec((None, None, rows, w), lambda bi, rho, i: (bi, rho, i, c))

    return pl.pallas_call(
        functools.partial(_dilated_attn_kernel, heads=heads, sub_len=sub_len, qb=qb),
        grid=(b, r, nb // qb),
        in_specs=[pl.BlockSpec(bias.shape, lambda bi, rho, i: (0, 0, 0)), row_spec(hw, 0)]
                 + [kv_spec(1, j) for j in range(qb + 2)] + [kv_spec(2, j) for j in range(qb + 2)],
        out_specs=[row_spec(hw, 0), row_spec(LANES, 0)],
        out_shape=[jax.ShapeDtypeStruct((b, r, sub_len, hw), BF16),
                   jax.ShapeDtypeStruct((b, r, sub_len, LANES), F32)],
        compiler_params=_params(("parallel", "parallel", "parallel")),
        name=f"dilated_attn_r{dilation}",
    )(bias, *([qkv] * (2 * qb + 5)))


def _dilated_combine_kernel(o0_ref, o1_ref, o2_ref, l0_ref, l1_ref, l2_ref, w_ref, x_ref, out_ref, out2_ref,
                            a_ref, oi_ref, li_ref, *, heads):
    for g, (o_ref, l_ref) in enumerate(((o0_ref, l0_ref), (o1_ref, l1_ref), (o2_ref, l2_ref))):
        r, per = o_ref.shape[0], o_ref.shape[1]
        for rho in range(r):
            rows = pl.ds(rho, per, stride=r) if r > 1 else slice(None)
            for h in range(heads):
                oi_ref[g, h, rows, :] = o_ref[rho, :, h * HEAD_DIM:(h + 1) * HEAD_DIM].astype(F32)
            li_ref[g, rows, :] = l_ref[rho]
    l0, l1, l2 = li_ref[0], li_ref[1], li_ref[2]
    m = jnp.maximum(jnp.maximum(l0, l1), l2)
    e0, e1, e2 = jnp.exp(l0 - m), jnp.exp(l1 - m), jnp.exp(l2 - m)
    inv = 1.0 / (e0 + e1 + e2)
    a0, a1, a2 = e0 * inv, e1 * inv, e2 * inv
    for h in range(heads):
        hs = slice(h * HEAD_DIM, (h + 1) * HEAD_DIM)
        o = a0[:, h:h + 1] * oi_ref[0, h] + a1[:, h:h + 1] * oi_ref[1, h] + a2[:, h:h + 1] * oi_ref[2, h]
        a_ref[:, hs] = o.astype(BF16)
    res = x_ref[...] + jnp.dot(a_ref[...], w_ref[...], preferred_element_type=F32)
    out_ref[...] = res
    out2_ref[...] = res


def _dilated_combine(os, ls, w_out, x, *, tm=512):
    b, s, d = x.shape
    hw = w_out.shape[0]
    tm = _pick(s, tm)
    per_b = s // tm

    def dspec(a):
        r, w = a.shape[1], a.shape[3]
        assert tm % (r * BF16_SUBLANES) == 0
        return pl.BlockSpec((None, r, tm // r, w), lambda i: (i // per_b, 0, i % per_b, 0))

    xspec = pl.BlockSpec((None, tm, d), lambda i: (i // per_b, i % per_b, 0))
    return pl.pallas_call(
        functools.partial(_dilated_combine_kernel, heads=hw // HEAD_DIM),
        grid=(b * per_b,),
        in_specs=[dspec(a) for a in os] + [dspec(a) for a in ls] + [pl.BlockSpec((hw, d), lambda i: (0, 0)), xspec],
        out_specs=[xspec, xspec],
        out_shape=[jax.ShapeDtypeStruct((b, s, d), F32)] * 2,
        scratch_shapes=[pltpu.VMEM((tm, hw), BF16), pltpu.VMEM((3, hw // HEAD_DIM, tm, HEAD_DIM), F32),
                        pltpu.VMEM((3, tm, LANES), F32)],
        compiler_params=_params(("parallel",)),
        name="dilated_combine",
    )(*os, *ls, w_out, x)


def _dilated_layer(x, g_norm, w_qkv, w_out, t5_bias):
    hw = w_out.shape[0]
    heads = hw // HEAD_DIM
    assert len(B_PAIRS) == 3 and w_qkv.shape[1] == 9 * hw
    os, ls = [], []
    for g, (window, dilation) in enumerate(B_PAIRS):
        assert window // (2 * dilation) == DIL_HALF
        qkv = _norm_matmul(x, g_norm, w_qkv, col0=g * 3 * hw, n=3 * hw, dilation=dilation)
        bias = _dilated_bias(t5_bias[:, g * heads:(g + 1) * heads], dilation)
        o, l = _dilated_attn(qkv, bias, dilation, heads)
        os.append(o)
        ls.append(l)
    return _dilated_combine(os, ls, w_out, x)


NA_QROWS = 8
NA_KBLK = 4
NA_WIN = NA_QROWS + NA_ROWS
NA_HEAD_GROUP = 4


def _na_bias_table(rpb):
    scale = HEAD_DIM ** -0.5
    qc = np.arange(GRID_W)[None, :]
    kc = np.arange(GRID_W)[:, None]
    cstart = np.clip(qc - NA_COLS // 2, 0, GRID_W - NA_COLS)
    ok = (kc >= cstart) & (kc < cstart + NA_COLS)
    dcol = np.clip(kc - qc, -(NA_COLS - 1), NA_COLS - 1) + NA_COLS - 1
    tab = _lookup_rows(jnp.moveaxis(rpb.astype(F32), 2, 0), dcol)
    tab = jnp.moveaxis(tab, (2, 3), (0, 1))
    tab = jnp.where(ok[None, None], tab, NEG_INF) / scale
    hi = tab.astype(BF16)
    lo = (tab - hi.astype(F32)).astype(BF16)
    out = jnp.concatenate([hi, lo], axis=-1)
    return out.reshape(rpb.shape[0], (2 * NA_ROWS - 1) * GRID_W, 2 * GRID_W)


def _na_attn_kernel(bk_ref, q_ref, k0_ref, k1_ref, k2_ref, k3_ref, v0_ref, v1_ref, v2_ref, v3_ref,
                    o_ref, kw_ref, vw_ref, bias_ref, *, heads, rows):
    rb = pl.program_id(1)
    blk = NA_KBLK * GRID_W
    for j, (kr, vr) in enumerate(((k0_ref, v0_ref), (k1_ref, v1_ref), (k2_ref, v2_ref), (k3_ref, v3_ref))):
        kw_ref[j * blk:(j + 1) * blk, :] = kr[...]
        vw_ref[j * blk:(j + 1) * blk, :] = vr[...]
    ws = NA_KBLK * jnp.clip(2 * rb - 1, 0, rows // NA_KBLK - NA_WIN // NA_KBLK)
    nkeys = NA_ROWS * GRID_W
    eye = (lax.broadcasted_iota(jnp.int32, (GRID_W, 2 * GRID_W), 0)
           == lax.broadcasted_iota(jnp.int32, (GRID_W, 2 * GRID_W), 1) % GRID_W)
    eye2 = jnp.where(eye, 1.0, 0.0).astype(BF16)
    scale = HEAD_DIM ** -0.5
    nt = (((1,), (1,)), ((), ()))

    def bias_of(h, bsl):
        return lax.dot_general(eye2, bk_ref[h, bsl, :], nt, preferred_element_type=F32)

    def row_loop(bias_is_staged):
        def row_body(rr, carry):
            r = rb * NA_QROWS + rr
            rs = jnp.clip(r - NA_ROWS // 2, 0, rows - NA_ROWS)
            ksl = pl.ds(pl.multiple_of((rs - ws) * GRID_W, GRID_W), nkeys)
            bsl = pl.ds(pl.multiple_of((rs - r + NA_ROWS - 1) * GRID_W, GRID_W), nkeys)
            qsl = pl.ds(pl.multiple_of(rr * GRID_W, GRID_W), GRID_W)
            for h0 in range(0, heads, NA_HEAD_GROUP):
                hss = [slice(h * HEAD_DIM, (h + 1) * HEAD_DIM) for h in range(h0, h0 + NA_HEAD_GROUP)]
                ss = [lax.dot_general(q_ref[qsl, hs], kw_ref[ksl, hs], nt, preferred_element_type=F32)
                      + (bias_ref[h0 + j] if bias_is_staged else bias_of(h0 + j, bsl))
                      for j, hs in enumerate(hss)]
                ps = []
                for s in ss:
                    s = s * scale
                    p = jnp.exp(s - jnp.max(s, axis=-1, keepdims=True))
                    ps.append((p * (1.0 / jnp.sum(p, axis=-1, keepdims=True))).astype(BF16))
                for p, hs in zip(ps, hss):
                    o_ref[qsl, hs] = jnp.dot(p, vw_ref[ksl, hs], preferred_element_type=F32).astype(o_ref.dtype)
            return carry
        lax.fori_loop(0, NA_QROWS, row_body, 0)

    interior = (rb > 0) & (rb < pl.num_programs(1) - 1)

    @pl.when(interior)
    def _():
        centre = pl.ds((NA_ROWS - 1 - NA_ROWS // 2) * GRID_W, nkeys)
        for h in range(heads):
            bias_ref[h] = bias_of(h, centre)
        row_loop(True)

    @pl.when(jnp.logical_not(interior))
    def _():
        row_loop(False)


def _na_attn(qkv, bk, heads):
    b, s, _ = qkv.shape
    hw = heads * HEAD_DIM
    rows = s // GRID_W
    assert s % GRID_W == 0 and rows % NA_QROWS == 0 and rows >= NA_WIN and 2 * GRID_W == LANES
    assert heads % NA_HEAD_GROUP == 0
    nwb = rows // NA_KBLK - NA_WIN // NA_KBLK
    blk = NA_KBLK * GRID_W

    def kv_spec(c, j):
        return pl.BlockSpec((None, blk, hw), lambda bi, rb: (bi, jnp.clip(2 * rb - 1, 0, nwb) + j, c))

    o = pl.pallas_call(
        functools.partial(_na_attn_kernel, heads=heads, rows=rows),
        grid=(b, rows // NA_QROWS),
        in_specs=[pl.BlockSpec(bk.shape, lambda bi, rb: (0, 0, 0)),
                  pl.BlockSpec((None, NA_QROWS * GRID_W, hw), lambda bi, rb: (bi, rb, 0))]
                 + [kv_spec(1, j) for j in range(4)] + [kv_spec(2, j) for j in range(4)],
        out_specs=pl.BlockSpec((None, NA_QROWS * GRID_W, hw), lambda bi, rb: (bi, rb, 0)),
        out_shape=jax.ShapeDtypeStruct((b, s, hw), BF16),
        scratch_shapes=[pltpu.VMEM((NA_WIN * GRID_W, hw), BF16), pltpu.VMEM((NA_WIN * GRID_W, hw), BF16),
                        pltpu.VMEM((heads, GRID_W, NA_ROWS * GRID_W), F32)],
        compiler_params=_params(("parallel", "parallel")),
        name="na_attn",
    )(bk, *([qkv] * 9))
    return o.reshape(b * s, hw)


def _neighbourhood_layer(x, g_norm, w_qkv, w_out, rpb):
    b, s, d = x.shape
    hw = w_out.shape[0]
    assert w_qkv.shape[1] == 3 * hw
    qkv = _norm_matmul(x, g_norm, w_qkv).reshape(b, s, 3 * hw)
    o = _na_attn(qkv, _na_bias_table(rpb), hw // HEAD_DIM)
    return [y.reshape(b, s, d) for y in _matmul_res(o, w_out, x.reshape(b * s, d))]


def _router_kernel(x_ref, g_ref, wt_ref, aff_ref):
    hn = _rmsnorm_rows(x_ref[...], g_ref[...]).astype(BF16)
    logits = lax.dot_general(wt_ref[...], hn, (((1,), (1,)), ((), ())), preferred_element_type=F32)
    m = jnp.max(logits, axis=0, keepdims=True)
    p = jnp.exp(logits - m)
    aff_ref[...] = p / jnp.sum(p, axis=0, keepdims=True)


def _router(x, g, w_router_t, *, tm=512):
    t, d = x.shape
    e = w_router_t.shape[0]
    tm = _pick(t, tm)
    return pl.pallas_call(
        _router_kernel,
        grid=(t // tm,),
        in_specs=[pl.BlockSpec((tm, d), lambda i: (i, 0)),
                  pl.BlockSpec((1, d), lambda i: (0, 0)),
                  pl.BlockSpec((e, d), lambda i: (0, 0))],
        out_specs=pl.BlockSpec((e, tm), lambda i: (0, i)),
        out_shape=jax.ShapeDtypeStruct((e, t), F32),
        compiler_params=_params(("parallel",)),
        name="moe_router",
    )(x, g.reshape(1, d), w_router_t)


def _topk_mask_kernel(aff_ref, sel_ref, *, cap):
    bits = pltpu.bitcast(aff_ref[...], jnp.int32)
    e, t = bits.shape

    def count(mask):
        return jnp.sum(jnp.where(mask, 1, 0), axis=1, keepdims=True)

    def value_step(_, c):
        lo, hi = c
        mid = lo + (hi - lo) // 2
        ok = count(bits >= mid) >= cap
        return jnp.where(ok, mid, lo), jnp.where(ok, hi, mid)

    lo0 = jnp.zeros((e, 1), jnp.int32)
    hi0 = jnp.full((e, 1), 0x7F800000, jnp.int32)
    thr, _ = lax.fori_loop(0, 31, value_step, (lo0, hi0))
    above = bits > thr
    tie = bits == thr
    need = cap - count(above)
    tok = lax.broadcasted_iota(jnp.int32, (e, t), 1)

    def index_step(_, c):
        lo, hi = c
        mid = lo + (hi - lo) // 2
        ok = count(tie & (tok < mid)) >= need
        return jnp.where(ok, lo, mid), jnp.where(ok, mid, hi)

    _, bound = lax.fori_loop(0, max(1, math.ceil(math.log2(t))), index_step,
                             (jnp.zeros((e, 1), jnp.int32), jnp.full((e, 1), t, jnp.int32)))
    sel_ref[...] = jnp.where(above | (tie & (tok < bound)), 1, 0)


def _topk_mask(aff_t, cap):
    e, t = aff_t.shape
    return pl.pallas_call(
        functools.partial(_topk_mask_kernel, cap=cap),
        grid=(1,),
        in_specs=[pl.BlockSpec((e, t), lambda i: (0, 0))],
        out_specs=pl.BlockSpec((e, t), lambda i: (0, 0)),
        out_shape=jax.ShapeDtypeStruct((e, t), jnp.int32),
        compiler_params=_params(("arbitrary",)),
        name="moe_topk_mask",
    )(aff_t)


def _compact_kernel(sel_ref, aff_ref, idx_ref, gate_ref, cnt_ref, bound_ref, piece_ref, *, tj):
    nb = sel_ref.shape[0]

    @pl.when(pl.program_id(1) == 0)
    def _():
        m = sel_ref[...].astype(F32).astype(BF16)
        tri = jnp.where(lax.broadcasted_iota(jnp.int32, (LANES, LANES), 0)
                        <= lax.broadcasted_iota(jnp.int32, (LANES, LANES), 1), 1.0, 0.0).astype(BF16)
        cnt_ref[...] = jnp.dot(m, tri, preferred_element_type=F32).astype(BF16)
        tot = lax.dot_general(jnp.ones((8, LANES), BF16), m, (((1,), (1,)), ((), ())),
                              preferred_element_type=F32)
        trib = jnp.where(lax.broadcasted_iota(jnp.int32, (nb, nb), 0)
                         <= lax.broadcasted_iota(jnp.int32, (nb, nb), 1), 1.0, 0.0).astype(BF16)
        incl = jnp.dot(tot.astype(BF16), trib, preferred_element_type=F32)
        bound_ref[0:8, :] = incl
        bound_ref[8:16, :] = incl - tot
        bits = pltpu.bitcast(aff_ref[...], jnp.int32)
        for k in range(4):
            piece_ref[k] = ((bits >> (8 * k)) & 0xFF).astype(F32).astype(BF16)

    jf = (pl.program_id(1) * tj + lax.broadcasted_iota(jnp.int32, (tj, 1), 0)).astype(F32)
    blk = jnp.sum(jnp.where(bound_ref[0:1, :] <= jf, 1, 0), axis=1, keepdims=True)
    onehot = lax.broadcasted_iota(jnp.int32, (tj, nb), 1) == blk
    start = jnp.sum(jnp.where(onehot, bound_ref[8:9, :], 0.0), axis=1, keepdims=True)
    oh = jnp.where(onehot, 1.0, 0.0).astype(BF16)
    cnt = jnp.dot(oh, cnt_ref[...], preferred_element_type=F32)
    lane = jnp.sum(jnp.where(cnt < jf - start + 1.0, 1, 0), axis=1, keepdims=True)
    idx_ref[...] = blk * LANES + lane
    pick = lax.broadcasted_iota(jnp.int32, (tj, LANES), 1) == lane
    bits = jnp.zeros((tj, 1), jnp.int32)
    for k in range(4):
        byte = jnp.dot(oh, piece_ref[k], preferred_element_type=F32)
        bits = bits | (jnp.sum(jnp.where(pick, byte, 0.0), axis=1, keepdims=True).astype(jnp.int32) << (8 * k))
    gate_ref[...] = pltpu.bitcast(bits, F32)


def _compact(sel, aff_t, cap, *, tj=512):
    e, t = sel.shape
    assert t % LANES == 0
    nb = t // LANES
    tj = _pick(cap, tj)
    nj = cap // tj
    blocks = pl.BlockSpec((None, nb, LANES), lambda ei, j: (ei, 0, 0))
    col = pl.BlockSpec((tj, 1), lambda ei, j: (ei * nj + j, 0))
    return pl.pallas_call(
        functools.partial(_compact_kernel, tj=tj),
        grid=(e, nj),
        in_specs=[blocks, blocks],
        out_specs=[col, col],
        out_shape=[jax.ShapeDtypeStruct((e * cap, 1), jnp.int32), jax.ShapeDtypeStruct((e * cap, 1), F32)],
        scratch_shapes=[pltpu.VMEM((nb, LANES), BF16), pltpu.VMEM((16, nb), F32), pltpu.VMEM((4, nb, LANES), BF16)],
        compiler_params=_params(("parallel", "arbitrary")),
        name="moe_compact",
    )(sel.reshape(e, nb, LANES), aff_t.reshape(e, nb, LANES))


X_SEM, OUT_SEM, SCATTER_SEM = 0, 1, 2


def _moe_ffn_kernel(idx_ref, gate_ref, gn_ref, wg_ref, wu_ref, wd_ref, x_hbm, yin_hbm, out_hbm,
                    xg_ref, ob_ref, h_ref, acc_ref, sem, *, tm, nt, nf, n_tiles):
    del yin_hbm
    i = pl.program_id(1)
    tile = pl.program_id(0) * nt + i
    base = tile * tm
    f = pl.program_id(2)
    slot = tile % 2
    first_of_expert = i == 0
    after_first = i == 1 % nt

    def row_copy(hbm, buf, first, r, s, to_vmem):
        tok = idx_ref[first + r]
        src, dst = hbm.at[pl.ds(tok, 1), :], buf.at[pl.ds(r, 1), :]
        return (pltpu.make_async_copy(src, dst, sem.at[s]) if to_vmem
                else pltpu.make_async_copy(dst, src, sem.at[s]))

    def start_rows(hbm, buf, first, s, to_vmem):
        def body(r, c):
            row_copy(hbm, buf, first, r, s, to_vmem).start()
            return c
        lax.fori_loop(0, tm, body, 0, unroll=32)

    def wait_rows(hbm, buf, s, to_vmem):
        whole = hbm.at[pl.ds(0, tm), :]
        (pltpu.make_async_copy(whole, buf, sem.at[s]) if to_vmem
         else pltpu.make_async_copy(buf, whole, sem.at[s])).wait()

    def wait_scatter(s):
        wait_rows(out_hbm, ob_ref.at[s], SCATTER_SEM + s, False)

    @pl.when(f == 0)
    def _():
        @pl.when(tile == 0)
        def _():
            start_rows(x_hbm, xg_ref, base, X_SEM, True)

        @pl.when((tile >= 2) & jnp.logical_not(after_first))
        def _():
            wait_scatter(slot)

        @pl.when((tile >= 1) & first_of_expert)
        def _():
            wait_scatter(1 - slot)

        wait_rows(x_hbm, xg_ref, X_SEM, True)
        h_ref[...] = _rmsnorm_rows(xg_ref[...], gn_ref[...]).astype(BF16)
        acc_ref[...] = jnp.zeros_like(acc_ref)

    nxt = jnp.minimum(tile + 1, n_tiles - 1) * tm
    per_step = tm // nf
    for r in range(per_step):
        row = f * per_step + r
        row_copy(out_hbm, ob_ref.at[slot], base, row, OUT_SEM, True).start()
        row_copy(x_hbm, xg_ref, nxt, row, X_SEM, True).start()

    h = h_ref[...]
    g = jnp.dot(h, wg_ref[...].astype(BF16), preferred_element_type=F32)
    u = jnp.dot(h, wu_ref[...].astype(BF16), preferred_element_type=F32)
    hid = (g * (1.0 / (1.0 + jnp.exp(-g))) * u).astype(BF16)
    acc_ref[...] += jnp.dot(hid, wd_ref[...].astype(BF16), preferred_element_type=F32)

    @pl.when(f == nf - 1)
    def _():
        wait_rows(out_hbm, ob_ref.at[slot], OUT_SEM, True)
        ob_ref[slot] = ob_ref[slot] + acc_ref[...] * gate_ref[...]
        start_rows(out_hbm, ob_ref.at[slot], base, SCATTER_SEM + slot, False)

        @pl.when(tile == n_tiles - 1)
        def _():
            @pl.when((tile >= 1) & jnp.logical_not(first_of_expert))
            def _():
                wait_scatter(1 - slot)

            wait_scatter(slot)
            wait_rows(x_hbm, xg_ref, X_SEM, True)


def _moe_ffn(x, x_copy, idx, gate, n_experts, g_norm, wg, wu, wd, layer, *, tm=1024, tf=256):
    t, d = x.shape
    e = n_experts
    cap = idx.shape[0] // e
    fdim = wg.shape[3]
    tm = _pick(cap, tm)
    tf = _pick(fdim, tf)
    nt, nf = cap // tm, fdim // tf
    assert tm % nf == 0
    grid_spec = pltpu.PrefetchScalarGridSpec(
        num_scalar_prefetch=1,
        grid=(e, nt, nf),
        in_specs=[pl.BlockSpec((tm, 1), lambda ei, i, f, idx: (ei * nt + i, 0)),
                  pl.BlockSpec((1, d), lambda ei, i, f, idx: (0, 0)),
                  pl.BlockSpec((None, None, d, tf), lambda ei, i, f, idx: (layer, ei, 0, f)),
                  pl.BlockSpec((None, None, d, tf), lambda ei, i, f, idx: (layer, ei, 0, f)),
                  pl.BlockSpec((None, None, tf, d), lambda ei, i, f, idx: (layer, ei, f, 0)),
                  pl.BlockSpec(memory_space=pl.ANY),
                  pl.BlockSpec(memory_space=pl.ANY)],
        out_specs=pl.BlockSpec(memory_space=pl.ANY),
        scratch_shapes=[pltpu.VMEM((tm, d), F32), pltpu.VMEM((2, tm, d), F32), pltpu.VMEM((tm, d), BF16),
                        pltpu.VMEM((tm, d), F32), pltpu.SemaphoreType.DMA((4,))],
    )
    return pl.pallas_call(
        functools.partial(_moe_ffn_kernel, tm=tm, nt=nt, nf=nf, n_tiles=e * nt),
        grid_spec=grid_spec,
        out_shape=jax.ShapeDtypeStruct((t, d), F32),
        input_output_aliases={7: 0},
        compiler_params=_params(("arbitrary", "arbitrary", "arbitrary")),
        name="moe_ffn",
    )(idx.reshape(e * cap), gate, g_norm.reshape(1, d), wg, wu, wd, x, x_copy)


def _moe_layer(x, x_copy, g_norm, w_router, wg, wu, wd, layer):
    b, s, d = x.shape
    t = b * s
    e = w_router.shape[1]
    cap = EC_CAPACITY * t // e
    xt = x.reshape(t, d)
    aff_t = _router(xt, g_norm, w_router.T.astype(BF16))
    sel = _topk_mask(aff_t, cap)
    idx, gate = _compact(sel, aff_t, cap)
    return _moe_ffn(xt, x_copy.reshape(t, d), idx, gate, e, g_norm, wg, wu, wd, layer).reshape(b, s, d)


def _trunk(x, w):
    for i in range(w["norm_mix"].shape[0]):
        kind, j = i % N_MIXERS, i // N_MIXERS
        if kind == 0:
            x, x2 = _fourier_layer(x, w["norm_mix"][i], w["a_w_in"][j], w["a_w_out"][j])
        elif kind == 1:
            x, x2 = _dilated_layer(x, w["norm_mix"][i], w["b_w_qkv"][j], w["b_w_out"][j], w["t5_bias"])
        else:
            x, x2 = _neighbourhood_layer(x, w["norm_mix"][i], w["c_w_qkv"][j], w["c_w_out"][j], w["c_rpb"][j])
        x = _moe_layer(x, x2, w["norm_ffn"][i], w["moe_router"][i], w["moe_w_gate"], w["moe_w_up"],
                       w["moe_w_down"], i)
    b, s, d = x.shape
    return _rmsnorm(x.reshape(b * s, d), w["norm_final"]).reshape(b, s, d)


def kernel(x_prompt, x_sample, norm_mix, norm_ffn, norm_final, a_w_in, a_w_out, b_w_qkv, b_w_out, t5_bias,
           c_w_qkv, c_w_out, c_rpb, moe_router, moe_w_gate, moe_w_up, moe_w_down):
    bf = lambda a: a.astype(BF16)
    w = dict(norm_mix=norm_mix, norm_ffn=norm_ffn, norm_final=norm_final, a_w_in=bf(a_w_in), a_w_out=bf(a_w_out),
             b_w_qkv=bf(b_w_qkv), b_w_out=bf(b_w_out), t5_bias=t5_bias, c_w_qkv=bf(c_w_qkv), c_w_out=bf(c_w_out),
             c_rpb=c_rpb, moe_router=moe_router, moe_w_gate=moe_w_gate, moe_w_up=moe_w_up, moe_w_down=moe_w_down)
    return _trunk(x_prompt, w), _trunk(x_sample, w)
```

```python
import functools
import math

import numpy as np
import jax
import jax.numpy as jnp
from jax import lax
from jax.experimental import pallas as pl
from jax.experimental.pallas import tpu as pltpu

F32 = jnp.float32
BF16 = jnp.bfloat16

RMS_EPS = 1e-6
NEG_INF = -1e30
HEAD_DIM = 128
GRID_W = 64
A_GROUPS = 4
B_PAIRS = ((128, 1), (512, 4), (2048, 16))
T5_BUCKETS = 32
T5_MAX_DIST = 1024
NA_ROWS = 8
NA_COLS = 16
EC_CAPACITY = 2
N_MIXERS = 3

V7X_VMEM_BYTES = 64 * 1024 * 1024
VMEM_LIMIT = V7X_VMEM_BYTES - 8 * 1024 * 1024
LANES = 128
BF16_SUBLANES = 16


def _params(sem):
    return pltpu.CompilerParams(dimension_semantics=sem, vmem_limit_bytes=VMEM_LIMIT)


def _pick(n, pref):
    t = min(n, pref)
    while n % t:
        t //= 2
    return t


def _rmsnorm_rows(x, g):
    ms = jnp.mean(x * x, axis=-1, keepdims=True)
    return (x * lax.rsqrt(ms + RMS_EPS)) * g


def _norm_matmul_kernel(x_ref, g_ref, w_ref, o_ref, h_ref, r_ref, *, rows, dilation):
    @pl.when(pl.program_id(1) == 0)
    def _():
        def body(c, carry):
            sl = pl.ds(pl.multiple_of(c * rows, rows), rows)
            h_ref[sl, :] = _rmsnorm_rows(x_ref[sl, :], g_ref[...]).astype(BF16)
            return carry
        lax.fori_loop(0, x_ref.shape[0] // rows, body, 0)

    res = jnp.dot(h_ref[...], w_ref[...], preferred_element_type=F32)
    if dilation == 1:
        o_ref[0] = res.astype(o_ref.dtype)
    else:
        chunks = [slice(c * LANES, (c + 1) * LANES) for c in range(r_ref.shape[0])]
        for c, cs in enumerate(chunks):
            r_ref[c] = res[:, cs]
        per = r_ref.shape[1] // dilation
        for rho in range(dilation):
            for c, cs in enumerate(chunks):
                o_ref[rho, :, cs] = r_ref[c, pl.ds(rho, per, stride=dilation), :].astype(o_ref.dtype)


def _norm_matmul(x, g, w, *, col0=0, n=None, dilation=1, tm=1024, tn=1024):
    b, s, d = x.shape
    n = w.shape[1] if n is None else n
    tm = _pick(s, tm)
    tn = _pick(n, tn)
    assert col0 % tn == 0 and tm % (dilation * BF16_SUBLANES) == 0
    rows = _pick(tm, 256)
    per_b = s // tm
    cb0 = col0 // tn
    out = pl.pallas_call(
        functools.partial(_norm_matmul_kernel, rows=rows, dilation=dilation),
        grid=(b * per_b, n // tn),
        in_specs=[pl.BlockSpec((None, tm, d), lambda i, j: (i // per_b, i % per_b, 0)),
                  pl.BlockSpec((1, d), lambda i, j: (0, 0)),
                  pl.BlockSpec((d, tn), lambda i, j: (0, cb0 + j))],
        out_specs=pl.BlockSpec((None, dilation, tm // dilation, tn), lambda i, j: (i // per_b, 0, i % per_b, j)),
        out_shape=jax.ShapeDtypeStruct((b, dilation, s // dilation, n), BF16),
        scratch_shapes=[pltpu.VMEM((tm, d), BF16),
                        pltpu.VMEM((tn // LANES, tm, LANES) if dilation > 1 else (1, 8, LANES), F32)],
        compiler_params=_params(("parallel", "arbitrary")),
        name="norm_matmul",
    )(x, g.reshape(1, d), w)
    return out


def _matmul_res_kernel(a_ref, w_ref, x_ref, o_ref, o2_ref):
    res = x_ref[...] + jnp.dot(a_ref[...], w_ref[...], preferred_element_type=F32)
    o_ref[...] = res
    o2_ref[...] = res


def _matmul_res(a, w, x, *, tm=512):
    t, k = a.shape
    d = w.shape[1]
    tm = _pick(t, tm)
    row = pl.BlockSpec((tm, d), lambda i: (i, 0))
    return pl.pallas_call(
        _matmul_res_kernel,
        grid=(t // tm,),
        in_specs=[pl.BlockSpec((tm, k), lambda i: (i, 0)), pl.BlockSpec((k, d), lambda i: (0, 0)), row],
        out_specs=[row, row],
        out_shape=[jax.ShapeDtypeStruct((t, d), F32)] * 2,
        compiler_params=_params(("parallel",)),
        name="matmul_res",
    )(a, w, x)


def _rmsnorm_kernel(x_ref, g_ref, o_ref):
    o_ref[...] = _rmsnorm_rows(x_ref[...], g_ref[...])


def _rmsnorm(x, g, *, tm=256):
    t, d = x.shape
    tm = _pick(t, tm)
    return pl.pallas_call(
        _rmsnorm_kernel,
        grid=(t // tm,),
        in_specs=[pl.BlockSpec((tm, d), lambda i: (i, 0)),
                  pl.BlockSpec((1, d), lambda i: (0, 0))],
        out_specs=pl.BlockSpec((tm, d), lambda i: (i, 0)),
        out_shape=jax.ShapeDtypeStruct((t, d), F32),
        compiler_params=_params(("parallel",)),
        name="final_rmsnorm",
    )(x, g.reshape(1, d))


FFT_S2 = 128
FFT_STAGE2_COLS = 1024


def _dft_tables(seq, ch):
    s1n = seq // FFT_S2
    k1 = np.arange(s1n, dtype=np.float64)
    ang1 = 2.0 * np.pi * np.outer(k1, k1) / s1n
    m1 = np.concatenate([np.cos(ang1), -np.sin(ang1)], axis=0)
    s2 = np.arange(FFT_S2, dtype=np.float64)
    kk = k1[:, None, None] + s1n * s2[None, :, None]
    ang2 = 2.0 * np.pi * ((kk * s2[None, None, :]) % seq) / seq
    gc, gs = np.cos(ang2), np.sin(ang2)
    g = np.concatenate([np.concatenate([gc, gs], axis=2),
                        np.concatenate([-gs, gc], axis=2)], axis=1)
    c = np.arange(ch, dtype=np.float64)
    angc = 2.0 * np.pi * (np.outer(c, c) % ch) / ch
    cs = np.concatenate([np.cos(angc), np.sin(angc)], axis=0) / math.sqrt(seq * ch)
    return (jnp.asarray(m1, F32).astype(BF16), jnp.asarray(g, F32).astype(BF16),
            jnp.asarray(cs, F32).astype(BF16))


def _fft_stage1_kernel(m_ref, u_ref, y_ref, uf_ref, yf_ref):
    s1n, sb, tn = u_ref.shape
    chunks = [slice(c * LANES, (c + 1) * LANES) for c in range(tn // LANES)]
    for c, cs in enumerate(chunks):
        uf_ref[c] = u_ref[:, :, cs].astype(F32).reshape(s1n * sb, LANES)
    for j in range(sb):
        rows = jnp.concatenate([uf_ref[c, pl.ds(j, s1n, stride=sb), :] for c in range(len(chunks))], axis=1)
        res = jnp.dot(m_ref[...], rows.astype(BF16), preferred_element_type=F32)
        for c, cs in enumerate(chunks):
            yf_ref[c, pl.ds(j, 2 * s1n, stride=sb), :] = res[:, cs]
    for c, cs in enumerate(chunks):
        y_ref[:, :, cs] = yf_ref[c].reshape(2 * s1n, sb, LANES).astype(y_ref.dtype)


def _fft_stage1(u, m1, *, tn=1024):
    b, s1n, _, d = u.shape
    tn = _pick(d, tn)
    sb = BF16_SUBLANES
    return pl.pallas_call(
        _fft_stage1_kernel,
        grid=(b, FFT_S2 // sb, d // tn),
        in_specs=[pl.BlockSpec((2 * s1n, s1n), lambda i, j, c: (0, 0)),
                  pl.BlockSpec((None, s1n, sb, tn), lambda i, j, c: (i, 0, j, c))],
        out_specs=pl.BlockSpec((None, 2 * s1n, sb, tn), lambda i, j, c: (i, 0, j, c)),
        out_shape=jax.ShapeDtypeStruct((b, 2 * s1n, FFT_S2, d), BF16),
        scratch_shapes=[pltpu.VMEM((tn // LANES, s1n * sb, LANES), F32),
                        pltpu.VMEM((tn // LANES, 2 * s1n * sb, LANES), F32)],
        compiler_params=_params(("parallel", "parallel", "parallel")),
        name="fft_stage1",
    )(m1, u)


def _fft_stage2_kernel(g_ref, yr_ref, yi_ref, cs_ref, f_ref, ff_ref):
    kb = g_ref.shape[0]
    ch = cs_ref.shape[1]
    for j in range(kb):
        y = jnp.concatenate([yr_ref[j], yi_ref[j]], axis=0)
        xc = jnp.dot(g_ref[j], y, preferred_element_type=F32).astype(BF16)
        for q in range(y.shape[1] // ch):
            cols = slice(q * ch, (q + 1) * ch)
            xg = jnp.concatenate([xc[:FFT_S2, cols], xc[FFT_S2:, cols]], axis=1)
            ff_ref[:, j, cols] = jnp.dot(xg, cs_ref[...], preferred_element_type=F32)
    f_ref[...] = ff_ref[...].astype(f_ref.dtype)


def _fft_stage2(y, g, cs):
    b, s1n2, _, d = y.shape
    s1n = s1n2 // 2
    ch = cs.shape[1]
    kb = BF16_SUBLANES
    assert s1n % kb == 0
    nk = s1n // kb
    tc = _pick(d, FFT_STAGE2_COLS)
    assert tc % ch == 0
    return pl.pallas_call(
        _fft_stage2_kernel,
        grid=(b, nk, d // tc),
        in_specs=[pl.BlockSpec((kb, 2 * FFT_S2, 2 * FFT_S2), lambda i, j, q: (j, 0, 0)),
                  pl.BlockSpec((None, kb, FFT_S2, tc), lambda i, j, q: (i, j, 0, q)),
                  pl.BlockSpec((None, kb, FFT_S2, tc), lambda i, j, q: (i, j + nk, 0, q)),
                  pl.BlockSpec(cs.shape, lambda i, j, q: (0, 0))],
        out_specs=pl.BlockSpec((None, FFT_S2, kb, tc), lambda i, j, q: (i, 0, j, q)),
        out_shape=jax.ShapeDtypeStruct((b, FFT_S2, s1n, d), BF16),
        scratch_shapes=[pltpu.VMEM((FFT_S2, kb, tc), F32)],
        compiler_params=_params(("parallel", "parallel", "parallel")),
        name="fft_stage2",
    )(g, y, y, cs)


def _fourier_layer(x, g_norm, w_in, w_out):
    b, s, d = x.shape
    assert s % FFT_S2 == 0 and d % A_GROUPS == 0
    s1n = s // FFT_S2
    m1, g, cs = _dft_tables(s, d // A_GROUPS)
    u = _norm_matmul(x, g_norm, w_in)
    y = _fft_stage1(u.reshape(b, s1n, FFT_S2, d), m1)
    f = _fft_stage2(y, g, cs)
    return [o.reshape(b, s, d) for o in _matmul_res(f.reshape(b * s, d), w_out, x.reshape(b * s, d))]


DIL_TQ = 128
DIL_HALF = 64
DIL_HEAD_GROUP = 4
DIL_QBLOCKS = 4


def _lookup_rows(table, index):
    onehot = jnp.asarray(np.eye(table.shape[0], dtype=np.float32)[index])
    return jnp.tensordot(onehot, table.astype(F32), axes=1, precision=lax.Precision.HIGHEST)


def _t5_bucket_np(rel):
    nb = T5_BUCKETS // 2
    max_exact = nb // 2
    n = np.abs(rel)
    large = max_exact + (np.log(np.maximum(n, 1) / max_exact) / math.log(T5_MAX_DIST / max_exact)
                         * (nb - max_exact)).astype(np.int32)
    large = np.minimum(large, nb - 1)
    return (np.where(rel > 0, nb, 0) + np.where(n < max_exact, n, large)).astype(np.int32)


def _dilated_bias(t5_bias_g, dilation):
    off = np.arange(DIL_TQ + 2 * DIL_HALF)[None, :] - DIL_HALF - np.arange(DIL_TQ)[:, None]
    band = np.abs(off) <= DIL_HALF
    bias = _lookup_rows(t5_bias_g, _t5_bucket_np(off * dilation)).transpose(2, 0, 1)
    return jnp.where(band[None], bias.astype(F32), NEG_INF)


def _dilated_attn_kernel(bias_ref, q_ref, *refs, heads, sub_len, qb):
    k_refs, v_refs = refs[:qb + 2], refs[qb + 2:2 * qb + 4]
    o_ref, l_ref = refs[2 * qb + 4:]
    i = pl.program_id(2)
    nk = DIL_TQ + 2 * DIL_HALF
    lane = lax.broadcasted_iota(jnp.int32, (DIL_TQ, LANES), 1)
    scale = HEAD_DIM ** -0.5

    def window(blocks, jb, hs):
        return jnp.concatenate([blocks[jb][DIL_TQ - DIL_HALF:, hs], blocks[jb + 1][:, hs],
                                blocks[jb + 2][:DIL_HALF, hs]], axis=0)

    for jb in range(qb):
        rows = slice(jb * DIL_TQ, (jb + 1) * DIL_TQ)
        kpos = (i * qb + jb) * DIL_TQ - DIL_HALF + lax.broadcasted_iota(jnp.int32, (1, nk), 1)
        valid = (kpos >= 0) & (kpos < sub_len)
        lse_all = jnp.zeros((DIL_TQ, LANES), F32)
        for h0 in range(0, heads, DIL_HEAD_GROUP):
            hss = [slice(h * HEAD_DIM, (h + 1) * HEAD_DIM) for h in range(h0, h0 + DIL_HEAD_GROUP)]
            ss = [lax.dot_general(q_ref[rows, hs], window(k_refs, jb, hs), (((1,), (1,)), ((), ())),
                                  preferred_element_type=F32) for hs in hss]
            ps = []
            for j, s in enumerate(ss):
                s = jnp.where(valid, s * scale + bias_ref[h0 + j], NEG_INF)
                m = jnp.max(s, axis=-1, keepdims=True)
                p = jnp.exp(s - m)
                den = jnp.sum(p, axis=-1, keepdims=True)
                ps.append((p * (1.0 / den)).astype(BF16))
                lse_all = jnp.where(lane == h0 + j, m + jnp.log(den), lse_all)
            for p, hs in zip(ps, hss):
                o = jnp.dot(p, window(v_refs, jb, hs), preferred_element_type=F32)
                o_ref[rows, hs] = o.astype(o_ref.dtype)
        l_ref[rows, :] = lse_all


def _dilated_attn(qkv, bias, dilation, heads):
    b, r, sub_len, _ = qkv.shape
    hw = heads * HEAD_DIM
    assert r == dilation and sub_len % DIL_TQ == 0
    nb = sub_len // DIL_TQ
    qb = _pick(nb, DIL_QBLOCKS)
    rows = qb * DIL_TQ

    def kv_spec(c, j):
        return pl.BlockSpec((None, None, DIL_TQ, hw),
                            lambda bi, rho, i: (bi, rho, jnp.clip(i * qb + j - 1, 0, nb - 1), c))

    def row_spec(w, c):
        return pl.BlockSpec((None, None, rows, w), lambda bi, rho, i: (bi, rho, i, c))

    return pl.pallas_call(
        functools.partial(_dilated_attn_kernel, heads=heads, sub_len=sub_len, qb=qb),
        grid=(b, r, nb // qb),
        in_specs=[pl.BlockSpec(bias.shape, lambda bi, rho, i: (0, 0, 0)), row_spec(hw, 0)]
                 + [kv_spec(1, j) for j in range(qb + 2)] + [kv_spec(2, j) for j in range(qb + 2)],
        out_specs=[row_spec(hw, 0), row_spec(LANES, 0)],
        out_shape=[jax.ShapeDtypeStruct((b, r, sub_len, hw), BF16),
                   jax.ShapeDtypeStruct((b, r, sub_len, LANES), F32)],
        compiler_params=_params(("parallel", "parallel", "parallel")),
        name=f"dilated_attn_r{dilation}",
    )(bias, *([qkv] * (2 * qb + 5)))


def _dilated_combine_kernel(o0_ref, o1_ref, o2_ref, l0_ref, l1_ref, l2_ref, w_ref, x_ref, out_ref, out2_ref,
                            a_ref, oi_ref, li_ref, *, heads):
    for g, (o_ref, l_ref) in enumerate(((o0_ref, l0_ref), (o1_ref, l1_ref), (o2_ref, l2_ref))):
        r, per = o_ref.shape[0], o_ref.shape[1]
        for rho in range(r):
            rows = pl.ds(rho, per, stride=r) if r > 1 else slice(None)
            for h in range(heads):
                oi_ref[g, h, rows, :] = o_ref[rho, :, h * HEAD_DIM:(h + 1) * HEAD_DIM].astype(F32)
            li_ref[g, rows, :] = l_ref[rho]
    l0, l1, l2 = li_ref[0], li_ref[1], li_ref[2]
    m = jnp.maximum(jnp.maximum(l0, l1), l2)
    e0, e1, e2 = jnp.exp(l0 - m), jnp.exp(l1 - m), jnp.exp(l2 - m)
    inv = 1.0 / (e0 + e1 + e2)
    a0, a1, a2 = e0 * inv, e1 * inv, e2 * inv
    for h in range(heads):
        hs = slice(h * HEAD_DIM, (h + 1) * HEAD_DIM)
        o = a0[:, h:h + 1] * oi_ref[0, h] + a1[:, h:h + 1] * oi_ref[1, h] + a2[:, h:h + 1] * oi_ref[2, h]
        a_ref[:, hs] = o.astype(BF16)
    res = x_ref[...] + jnp.dot(a_ref[...], w_ref[...], preferred_element_type=F32)
    out_ref[...] = res
    out2_ref[...] = res


def _dilated_combine(os, ls, w_out, x, *, tm=512):
    b, s, d = x.shape
    hw = w_out.shape[0]
    tm = _pick(s, tm)
    per_b = s // tm

    def dspec(a):
        r, w = a.shape[1], a.shape[3]
        assert tm % (r * BF16_SUBLANES) == 0
        return pl.BlockSpec((None, r, tm // r, w), lambda i: (i // per_b, 0, i % per_b, 0))

    xspec = pl.BlockSpec((None, tm, d), lambda i: (i // per_b, i % per_b, 0))
    return pl.pallas_call(
        functools.partial(_dilated_combine_kernel, heads=hw // HEAD_DIM),
        grid=(b * per_b,),
        in_specs=[dspec(a) for a in os] + [dspec(a) for a in ls] + [pl.BlockSpec((hw, d), lambda i: (0, 0)), xspec],
        out_specs=[xspec, xspec],
        out_shape=[jax.ShapeDtypeStruct((b, s, d), F32)] * 2,
        scratch_shapes=[pltpu.VMEM((tm, hw), BF16), pltpu.VMEM((3, hw // HEAD_DIM, tm, HEAD_DIM), F32),
                        pltpu.VMEM((3, tm, LANES), F32)],
        compiler_params=_params(("parallel",)),
        name="dilated_combine",
    )(*os, *ls, w_out, x)


def _dilated_layer(x, g_norm, w_qkv, w_out, t5_bias):
    hw = w_out.shape[0]
    heads = hw // HEAD_DIM
    assert len(B_PAIRS) == 3 and w_qkv.shape[1] == 9 * hw
    os, ls = [], []
    for g, (window, dilation) in enumerate(B_PAIRS):
        assert window // (2 * dilation) == DIL_HALF
        qkv = _norm_matmul(x, g_norm, w_qkv, col0=g * 3 * hw, n=3 * hw, dilation=dilation)
        bias = _dilated_bias(t5_bias[:, g * heads:(g + 1) * heads], dilation)
        o, l = _dilated_attn(qkv, bias, dilation, heads)
        os.append(o)
        ls.append(l)
    return _dilated_combine(os, ls, w_out, x)


NA_QROWS = 8
NA_KBLK = 4
NA_WIN = NA_QROWS + NA_ROWS
NA_HEAD_GROUP = 4


def _na_bias_table(rpb):
    scale = HEAD_DIM ** -0.5
    qc = np.arange(GRID_W)[None, :]
    kc = np.arange(GRID_W)[:, None]
    cstart = np.clip(qc - NA_COLS // 2, 0, GRID_W - NA_COLS)
    ok = (kc >= cstart) & (kc < cstart + NA_COLS)
    dcol = np.clip(kc - qc, -(NA_COLS - 1), NA_COLS - 1) + NA_COLS - 1
    tab = _lookup_rows(jnp.moveaxis(rpb.astype(F32), 2, 0), dcol)
    tab = jnp.moveaxis(tab, (2, 3), (0, 1))
    tab = jnp.where(ok[None, None], tab, NEG_INF) / scale
    hi = tab.astype(BF16)
    lo = (tab - hi.astype(F32)).astype(BF16)
    out = jnp.concatenate([hi, lo], axis=-1)
    return out.reshape(rpb.shape[0], (2 * NA_ROWS - 1) * GRID_W, 2 * GRID_W)


def _na_attn_kernel(bk_ref, q_ref, k0_ref, k1_ref, k2_ref, k3_ref, v0_ref, v1_ref, v2_ref, v3_ref,
                    o_ref, kw_ref, vw_ref, bias_ref, *, heads, rows):
    rb = pl.program_id(1)
    blk = NA_KBLK * GRID_W
    for j, (kr, vr) in enumerate(((k0_ref, v0_ref), (k1_ref, v1_ref), (k2_ref, v2_ref), (k3_ref, v3_ref))):
        kw_ref[j * blk:(j + 1) * blk, :] = kr[...]
        vw_ref[j * blk:(j + 1) * blk, :] = vr[...]
    ws = NA_KBLK * jnp.clip(2 * rb - 1, 0, rows // NA_KBLK - NA_WIN // NA_KBLK)
    nkeys = NA_ROWS * GRID_W
    eye = (lax.broadcasted_iota(jnp.int32, (GRID_W, 2 * GRID_W), 0)
           == lax.broadcasted_iota(jnp.int32, (GRID_W, 2 * GRID_W), 1) % GRID_W)
    eye2 = jnp.where(eye, 1.0, 0.0).astype(BF16)
    scale = HEAD_DIM ** -0.5
    nt = (((1,), (1,)), ((), ()))

    def bias_of(h, bsl):
        return lax.dot_general(eye2, bk_ref[h, bsl, :], nt, preferred_element_type=F32)

    def row_loop(bias_is_staged):
        def row_body(rr, carry):
            r = rb * NA_QROWS + rr
            rs = jnp.clip(r - NA_ROWS // 2, 0, rows - NA_ROWS)
            ksl = pl.ds(pl.multiple_of((rs - ws) * GRID_W, GRID_W), nkeys)
            bsl = pl.ds(pl.multiple_of((rs - r + NA_ROWS - 1) * GRID_W, GRID_W), nkeys)
            qsl = pl.ds(pl.multiple_of(rr * GRID_W, GRID_W), GRID_W)
            for h0 in range(0, heads, NA_HEAD_GROUP):
                hss = [slice(h * HEAD_DIM, (h + 1) * HEAD_DIM) for h in range(h0, h0 + NA_HEAD_GROUP)]
                ss = [lax.dot_general(q_ref[qsl, hs], kw_ref[ksl, hs], nt, preferred_element_type=F32)
                      + (bias_ref[h0 + j] if bias_is_staged else bias_of(h0 + j, bsl))
                      for j, hs in enumerate(hss)]
                ps = []
                for s in ss:
                    s = s * scale
                    p = jnp.exp(s - jnp.max(s, axis=-1, keepdims=True))
                    ps.append((p * (1.0 / jnp.sum(p, axis=-1, keepdims=True))).astype(BF16))
                for p, hs in zip(ps, hss):
                    o_ref[qsl, hs] = jnp.dot(p, vw_ref[ksl, hs], preferred_element_type=F32).astype(o_ref.dtype)
            return carry
        lax.fori_loop(0, NA_QROWS, row_body, 0)

    interior = (rb > 0) & (rb < pl.num_programs(1) - 1)

    @pl.when(interior)
    def _():
        centre = pl.ds((NA_ROWS - 1 - NA_ROWS // 2) * GRID_W, nkeys)
        for h in range(heads):
            bias_ref[h] = bias_of(h, centre)
        row_loop(True)

    @pl.when(jnp.logical_not(interior))
    def _():
        row_loop(False)


def _na_attn(qkv, bk, heads):
    b, s, _ = qkv.shape
    hw = heads * HEAD_DIM
    rows = s // GRID_W
    assert s % GRID_W == 0 and rows % NA_QROWS == 0 and rows >= NA_WIN and 2 * GRID_W == LANES
    assert heads % NA_HEAD_GROUP == 0
    nwb = rows // NA_KBLK - NA_WIN // NA_KBLK
    blk = NA_KBLK * GRID_W

    def kv_spec(c, j):
        return pl.BlockSpec((None, blk, hw), lambda bi, rb: (bi, jnp.clip(2 * rb - 1, 0, nwb) + j, c))

    o = pl.pallas_call(
        functools.partial(_na_attn_kernel, heads=heads, rows=rows),
        grid=(b, rows // NA_QROWS),
        in_specs=[pl.BlockSpec(bk.shape, lambda bi, rb: (0, 0, 0)),
                  pl.BlockSpec((None, NA_QROWS * GRID_W, hw), lambda bi, rb: (bi, rb, 0))]
                 + [kv_spec(1, j) for j in range(4)] + [kv_spec(2, j) for j in range(4)],
        out_specs=pl.BlockSpec((None, NA_QROWS * GRID_W, hw), lambda bi, rb: (bi, rb, 0)),
        out_shape=jax.ShapeDtypeStruct((b, s, hw), BF16),
        scratch_shapes=[pltpu.VMEM((NA_WIN * GRID_W, hw), BF16), pltpu.VMEM((NA_WIN * GRID_W, hw), BF16),
                        pltpu.VMEM((heads, GRID_W, NA_ROWS * GRID_W), F32)],
        compiler_params=_params(("parallel", "parallel")),
        name="na_attn",
    )(bk, *([qkv] * 9))
    return o.reshape(b * s, hw)


def _neighbourhood_layer(x, g_norm, w_qkv, w_out, rpb):
    b, s, d = x.shape
    hw = w_out.shape[0]
    assert w_qkv.shape[1] == 3 * hw
    qkv = _norm_matmul(x, g_norm, w_qkv).reshape(b, s, 3 * hw)
    o = _na_attn(qkv, _na_bias_table(rpb), hw // HEAD_DIM)
    return [y.reshape(b, s, d) for y in _matmul_res(o, w_out, x.reshape(b * s, d))]


def _router_kernel(x_ref, g_ref, wt_ref, aff_ref):
    hn = _rmsnorm_rows(x_ref[...], g_ref[...]).astype(BF16)
    logits = lax.dot_general(wt_ref[...], hn, (((1,), (1,)), ((), ())), preferred_element_type=F32)
    m = jnp.max(logits, axis=0, keepdims=True)
    p = jnp.exp(logits - m)
    aff_ref[...] = p / jnp.sum(p, axis=0, keepdims=True)


def _router(x, g, w_router_t, *, tm=512):
    t, d = x.shape
    e = w_router_t.shape[0]
    tm = _pick(t, tm)
    return pl.pallas_call(
        _router_kernel,
        grid=(t // tm,),
        in_specs=[pl.BlockSpec((tm, d), lambda i: (i, 0)),
                  pl.BlockSpec((1, d), lambda i: (0, 0)),
                  pl.BlockSpec((e, d), lambda i: (0, 0))],
        out_specs=pl.BlockSpec((e, tm), lambda i: (0, i)),
        out_shape=jax.ShapeDtypeStruct((e, t), F32),
        compiler_params=_params(("parallel",)),
        name="moe_router",
    )(x, g.reshape(1, d), w_router_t)


def _topk_mask_kernel(aff_ref, sel_ref, *, cap):
    bits = pltpu.bitcast(aff_ref[...], jnp.int32)
    e, t = bits.shape

    def count(mask):
        return jnp.sum(jnp.where(mask, 1, 0), axis=1, keepdims=True)

    def value_step(_, c):
        lo, hi = c
        mid = lo + (hi - lo) // 2
        ok = count(bits >= mid) >= cap
        return jnp.where(ok, mid, lo), jnp.where(ok, hi, mid)

    lo0 = jnp.zeros((e, 1), jnp.int32)
    hi0 = jnp.full((e, 1), 0x7F800000, jnp.int32)
    thr, _ = lax.fori_loop(0, 31, value_step, (lo0, hi0))
    above = bits > thr
    tie = bits == thr
    need = cap - count(above)
    tok = lax.broadcasted_iota(jnp.int32, (e, t), 1)

    def index_step(_, c):
        lo, hi = c
        mid = lo + (hi - lo) // 2
        ok = count(tie & (tok < mid)) >= need
        return jnp.where(ok, lo, mid), jnp.where(ok, mid, hi)

    _, bound = lax.fori_loop(0, max(1, math.ceil(math.log2(t))), index_step,
                             (jnp.zeros((e, 1), jnp.int32), jnp.full((e, 1), t, jnp.int32)))
    sel_ref[...] = jnp.where(above | (tie & (tok < bound)), 1, 0)


def _topk_mask(aff_t, cap):
    e, t = aff_t.shape
    return pl.pallas_call(
        functools.partial(_topk_mask_kernel, cap=cap),
        grid=(1,),
        in_specs=[pl.BlockSpec((e, t), lambda i: (0, 0))],
        out_specs=pl.BlockSpec((e, t), lambda i: (0, 0)),
        out_shape=jax.ShapeDtypeStruct((e, t), jnp.int32),
        compiler_params=_params(("arbitrary",)),
        name="moe_topk_mask",
    )(aff_t)


def _compact_kernel(sel_ref, aff_ref, idx_ref, gate_ref, cnt_ref, bound_ref, piece_ref, *, tj):
    nb = sel_ref.shape[0]

    @pl.when(pl.program_id(1) == 0)
    def _():
        m = sel_ref[...].astype(F32).astype(BF16)
        tri = jnp.where(lax.broadcasted_iota(jnp.int32, (LANES, LANES), 0)
                        <= lax.broadcasted_iota(jnp.int32, (LANES, LANES), 1), 1.0, 0.0).astype(BF16)
        cnt_ref[...] = jnp.dot(m, tri, preferred_element_type=F32).astype(BF16)
        tot = lax.dot_general(jnp.ones((8, LANES), BF16), m, (((1,), (1,)), ((), ())),
                              preferred_element_type=F32)
        trib = jnp.where(lax.broadcasted_iota(jnp.int32, (nb, nb), 0)
                         <= lax.broadcasted_iota(jnp.int32, (nb, nb), 1), 1.0, 0.0).astype(BF16)
        incl = jnp.dot(tot.astype(BF16), trib, preferred_element_type=F32)
        bound_ref[0:8, :] = incl
        bound_ref[8:16, :] = incl - tot
        bits = pltpu.bitcast(aff_ref[...], jnp.int32)
        for k in range(4):
            piece_ref[k] = ((bits >> (8 * k)) & 0xFF).astype(F32).astype(BF16)

    jf = (pl.program_id(1) * tj + lax.broadcasted_iota(jnp.int32, (tj, 1), 0)).astype(F32)
    blk = jnp.sum(jnp.where(bound_ref[0:1, :] <= jf, 1, 0), axis=1, keepdims=True)
    onehot = lax.broadcasted_iota(jnp.int32, (tj, nb), 1) == blk
    start = jnp.sum(jnp.where(onehot, bound_ref[8:9, :], 0.0), axis=1, keepdims=True)
    oh = jnp.where(onehot, 1.0, 0.0).astype(BF16)
    cnt = jnp.dot(oh, cnt_ref[...], preferred_element_type=F32)
    lane = jnp.sum(jnp.where(cnt < jf - start + 1.0, 1, 0), axis=1, keepdims=True)
    idx_ref[...] = blk * LANES + lane
    pick = lax.broadcasted_iota(jnp.int32, (tj, LANES), 1) == lane
    bits = jnp.zeros((tj, 1), jnp.int32)
    for k in range(4):
        byte = jnp.dot(oh, piece_ref[k], preferred_element_type=F32)
        bits = bits | (jnp.sum(jnp.where(pick, byte, 0.0), axis=1, keepdims=True).astype(jnp.int32) << (8 * k))
    gate_ref[...] = pltpu.bitcast(bits, F32)


def _compact(sel, aff_t, cap, *, tj=512):
    e, t = sel.shape
    assert t % LANES == 0
    nb = t // LANES
    tj = _pick(cap, tj)
    nj = cap // tj
    blocks = pl.BlockSpec((None, nb, LANES), lambda ei, j: (ei, 0, 0))
    col = pl.BlockSpec((tj, 1), lambda ei, j: (ei * nj + j, 0))
    return pl.pallas_call(
        functools.partial(_compact_kernel, tj=tj),
        grid=(e, nj),
        in_specs=[blocks, blocks],
        out_specs=[col, col],
        out_shape=[jax.ShapeDtypeStruct((e * cap, 1), jnp.int32), jax.ShapeDtypeStruct((e * cap, 1), F32)],
        scratch_shapes=[pltpu.VMEM((nb, LANES), BF16), pltpu.VMEM((16, nb), F32), pltpu.VMEM((4, nb, LANES), BF16)],
        compiler_params=_params(("parallel", "arbitrary")),
        name="moe_compact",
    )(sel.reshape(e, nb, LANES), aff_t.reshape(e, nb, LANES))


X_SEM, OUT_SEM, SCATTER_SEM = 0, 1, 2


def _moe_ffn_kernel(idx_ref, gate_ref, gn_ref, wg_ref, wu_ref, wd_ref, x_hbm, yin_hbm, out_hbm,
                    xg_ref, ob_ref, h_ref, acc_ref, sem, *, tm, nt, nf, n_tiles):
    del yin_hbm
    i = pl.program_id(1)
    tile = pl.program_id(0) * nt + i
    base = tile * tm
    f = pl.program_id(2)
    slot = tile % 2
    first_of_expert = i == 0
    after_first = i == 1 % nt

    def row_copy(hbm, buf, first, r, s, to_vmem):
        tok = idx_ref[first + r]
        src, dst = hbm.at[pl.ds(tok, 1), :], buf.at[pl.ds(r, 1), :]
        return (pltpu.make_async_copy(src, dst, sem.at[s]) if to_vmem
                else pltpu.make_async_copy(dst, src, sem.at[s]))

    def start_rows(hbm, buf, first, s, to_vmem):
        def body(r, c):
            row_copy(hbm, buf, first, r, s, to_vmem).start()
            return c
        lax.fori_loop(0, tm, body, 0, unroll=32)

    def wait_rows(hbm, buf, s, to_vmem):
        whole = hbm.at[pl.ds(0, tm), :]
        (pltpu.make_async_copy(whole, buf, sem.at[s]) if to_vmem
         else pltpu.make_async_copy(buf, whole, sem.at[s])).wait()

    def wait_scatter(s):
        wait_rows(out_hbm, ob_ref.at[s], SCATTER_SEM + s, False)

    @pl.when(f == 0)
    def _():
        @pl.when(tile == 0)
        def _():
            start_rows(x_hbm, xg_ref, base, X_SEM, True)

        @pl.when((tile >= 2) & jnp.logical_not(after_first))
        def _():
            wait_scatter(slot)

        @pl.when((tile >= 1) & first_of_expert)
        def _():
            wait_scatter(1 - slot)

        wait_rows(x_hbm, xg_ref, X_SEM, True)
        h_ref[...] = _rmsnorm_rows(xg_ref[...], gn_ref[...]).astype(BF16)
        acc_ref[...] = jnp.zeros_like(acc_ref)

    nxt = jnp.minimum(tile + 1, n_tiles - 1) * tm
    per_step = tm // nf
    for r in range(per_step):
        row = f * per_step + r
        row_copy(out_hbm, ob_ref.at[slot], base, row, OUT_SEM, True).start()
        row_copy(x_hbm, xg_ref, nxt, row, X_SEM, True).start()

    h = h_ref[...]
    g = jnp.dot(h, wg_ref[...].astype(BF16), preferred_element_type=F32)
    u = jnp.dot(h, wu_ref[...].astype(BF16), preferred_element_type=F32)
    hid = (g * (1.0 / (1.0 + jnp.exp(-g))) * u).astype(BF16)
    acc_ref[...] += jnp.dot(hid, wd_ref[...].astype(BF16), preferred_element_type=F32)

    @pl.when(f == nf - 1)
    def _():
        wait_rows(out_hbm, ob_ref.at[slot], OUT_SEM, True)
        ob_ref[slot] = ob_ref[slot] + acc_ref[...] * gate_ref[...]
        start_rows(out_hbm, ob_ref.at[slot], base, SCATTER_SEM + slot, False)

        @pl.when(tile == n_tiles - 1)
        def _():
            @pl.when((tile >= 1) & jnp.logical_not(first_of_expert))
            def _():
                wait_scatter(1 - slot)

            wait_scatter(slot)
            wait_rows(x_hbm, xg_ref, X_SEM, True)


def _moe_ffn(x, x_copy, idx, gate, n_experts, g_norm, wg, wu, wd, layer, *, tm=1024, tf=256):
    t, d = x.shape
    e = n_experts
    cap = idx.shape[0] // e
    fdim = wg.shape[3]
    tm = _pick(cap, tm)
    tf = _pick(fdim, tf)
    nt, nf = cap // tm, fdim // tf
    assert tm % nf == 0
    grid_spec = pltpu.PrefetchScalarGridSpec(
        num_scalar_prefetch=1,
        grid=(e, nt, nf),
        in_specs=[pl.BlockSpec((tm, 1), lambda ei, i, f, idx: (ei * nt + i, 0)),
                  pl.BlockSpec((1, d), lambda ei, i, f, idx: (0, 0)),
                  pl.BlockSpec((None, None, d, tf), lambda ei, i, f, idx: (layer, ei, 0, f)),
                  pl.BlockSpec((None, None, d, tf), lambda ei, i, f, idx: (layer, ei, 0, f)),
                  pl.BlockSpec((None, None, tf, d), lambda ei, i, f, idx: (layer, ei, f, 0)),
                  pl.BlockSpec(memory_space=pl.ANY),
                  pl.BlockSpec(memory_space=pl.ANY)],
        out_specs=pl.BlockSpec(memory_space=pl.ANY),
        scratch_shapes=[pltpu.VMEM((tm, d), F32), pltpu.VMEM((2, tm, d), F32), pltpu.VMEM((tm, d), BF16),
                        pltpu.VMEM((tm, d), F32), pltpu.SemaphoreType.DMA((4,))],
    )
    return pl.pallas_call(
        functools.partial(_moe_ffn_kernel, tm=tm, nt=nt, nf=nf, n_tiles=e * nt),
        grid_spec=grid_spec,
        out_shape=jax.ShapeDtypeStruct((t, d), F32),
        input_output_aliases={7: 0},
        compiler_params=_params(("arbitrary", "arbitrary", "arbitrary")),
        name="moe_ffn",
    )(idx.reshape(e * cap), gate, g_norm.reshape(1, d), wg, wu, wd, x, x_copy)


def _moe_layer(x, x_copy, g_norm, w_router, wg, wu, wd, layer):
    b, s, d = x.shape
    t = b * s
    e = w_router.shape[1]
    cap = EC_CAPACITY * t // e
    xt = x.reshape(t, d)
    aff_t = _router(xt, g_norm, w_router.T.astype(BF16))
    sel = _topk_mask(aff_t, cap)
    idx, gate = _compact(sel, aff_t, cap)
    return _moe_ffn(xt, x_copy.reshape(t, d), idx, gate, e, g_norm, wg, wu, wd, layer).reshape(b, s, d)


def _trunk(x, w):
    for i in range(w["norm_mix"].shape[0]):
        kind, j = i % N_MIXERS, i // N_MIXERS
        if kind == 0:
            x, x2 = _fourier_layer(x, w["norm_mix"][i], w["a_w_in"][j], w["a_w_out"][j])
        elif kind == 1:
            x, x2 = _dilated_layer(x, w["norm_mix"][i], w["b_w_qkv"][j], w["b_w_out"][j], w["t5_bias"])
        else:
            x, x2 = _neighbourhood_layer(x, w["norm_mix"][i], w["c_w_qkv"][j], w["c_w_out"][j], w["c_rpb"][j])
        x = _moe_layer(x, x2, w["norm_ffn"][i], w["moe_router"][i], w["moe_w_gate"], w["moe_w_up"],
                       w["moe_w_down"], i)
    b, s, d = x.shape
    return _rmsnorm(x.reshape(b * s, d), w["norm_final"]).reshape(b, s, d)


def kernel(x_prompt, x_sample, norm_mix, norm_ffn, norm_final, a_w_in, a_w_out, b_w_qkv, b_w_out, t5_bias,
           c_w_qkv, c_w_out, c_rpb, moe_router, moe_w_gate, moe_w_up, moe_w_down):
    bf = lambda a: a.astype(BF16)
    w = dict(norm_mix=norm_mix, norm_ffn=norm_ffn, norm_final=norm_final, a_w_in=bf(a_w_in), a_w_out=bf(a_w_out),
             b_w_qkv=bf(b_w_qkv), b_w_out=bf(b_w_out), t5_bias=t5_bias, c_w_qkv=bf(c_w_qkv), c_w_out=bf(c_w_out),
             c_rpb=c_rpb, moe_router=moe_router, moe_w_gate=moe_w_gate, moe_w_up=moe_w_up, moe_w_down=moe_w_down)
    return _trunk(x_prompt, w), _trunk(x_sample, w)
```

```python
import functools
import math

import numpy as np
import jax
import jax.numpy as jnp
from jax import lax
from jax.experimental import pallas as pl
from jax.experimental.pallas import tpu as pltpu

F32 = jnp.float32
BF16 = jnp.bfloat16

RMS_EPS = 1e-6
NEG_INF = -1e30
HEAD_DIM = 128
GRID_W = 64
A_GROUPS = 4
B_PAIRS = ((128, 1), (512, 4), (2048, 16))
T5_BUCKETS = 32
T5_MAX_DIST = 1024
NA_ROWS = 8
NA_COLS = 16
EC_CAPACITY = 2
N_MIXERS = 3

V7X_VMEM_BYTES = 64 * 1024 * 1024
VMEM_LIMIT = V7X_VMEM_BYTES - 8 * 1024 * 1024
LANES = 128
BF16_SUBLANES = 16


def _params(sem):
    return pltpu.CompilerParams(dimension_semantics=sem, vmem_limit_bytes=VMEM_LIMIT)


def _pick(n, pref):
    t = min(n, pref)
    while n % t:
        t //= 2
    return t


def _rmsnorm_rows(x, g):
    ms = jnp.mean(x * x, axis=-1, keepdims=True)
    return (x * lax.rsqrt(ms + RMS_EPS)) * g


def _norm_matmul_kernel(x_ref, g_ref, w_ref, o_ref, h_ref, r_ref, *, rows, dilation):
    @pl.when(pl.program_id(1) == 0)
    def _():
        def body(c, carry):
            sl = pl.ds(pl.multiple_of(c * rows, rows), rows)
            h_ref[sl, :] = _rmsnorm_rows(x_ref[sl, :], g_ref[...]).astype(BF16)
            return carry
        lax.fori_loop(0, x_ref.shape[0] // rows, body, 0)

    res = jnp.dot(h_ref[...], w_ref[...], preferred_element_type=F32)
    if dilation == 1:
        o_ref[0] = res.astype(o_ref.dtype)
    else:
        chunks = [slice(c * LANES, (c + 1) * LANES) for c in range(r_ref.shape[0])]
        for c, cs in enumerate(chunks):
            r_ref[c] = res[:, cs]
        per = r_ref.shape[1] // dilation
        for rho in range(dilation):
            for c, cs in enumerate(chunks):
                o_ref[rho, :, cs] = r_ref[c, pl.ds(rho, per, stride=dilation), :].astype(o_ref.dtype)


def _norm_matmul(x, g, w, *, col0=0, n=None, dilation=1, tm=1024, tn=1024):
    b, s, d = x.shape
    n = w.shape[1] if n is None else n
    tm = _pick(s, tm)
    tn = _pick(n, tn)
    assert col0 % tn == 0 and tm % (dilation * BF16_SUBLANES) == 0
    rows = _pick(tm, 256)
    per_b = s // tm
    cb0 = col0 // tn
    out = pl.pallas_call(
        functools.partial(_norm_matmul_kernel, rows=rows, dilation=dilation),
        grid=(b * per_b, n // tn),
        in_specs=[pl.BlockSpec((None, tm, d), lambda i, j: (i // per_b, i % per_b, 0)),
                  pl.BlockSpec((1, d), lambda i, j: (0, 0)),
                  pl.BlockSpec((d, tn), lambda i, j: (0, cb0 + j))],
        out_specs=pl.BlockSpec((None, dilation, tm // dilation, tn), lambda i, j: (i // per_b, 0, i % per_b, j)),
        out_shape=jax.ShapeDtypeStruct((b, dilation, s // dilation, n), BF16),
        scratch_shapes=[pltpu.VMEM((tm, d), BF16),
                        pltpu.VMEM((tn // LANES, tm, LANES) if dilation > 1 else (1, 8, LANES), F32)],
        compiler_params=_params(("parallel", "arbitrary")),
        name="norm_matmul",
    )(x, g.reshape(1, d), w)
    return out


def _matmul_res_kernel(a_ref, w_ref, x_ref, o_ref, o2_ref):
    res = x_ref[...] + jnp.dot(a_ref[...], w_ref[...], preferred_element_type=F32)
    o_ref[...] = res
    o2_ref[...] = res


def _matmul_res(a, w, x, *, tm=512):
    t, k = a.shape
    d = w.shape[1]
    tm = _pick(t, tm)
    row = pl.BlockSpec((tm, d), lambda i: (i, 0))
    return pl.pallas_call(
        _matmul_res_kernel,
        grid=(t // tm,),
        in_specs=[pl.BlockSpec((tm, k), lambda i: (i, 0)), pl.BlockSpec((k, d), lambda i: (0, 0)), row],
        out_specs=[row, row],
        out_shape=[jax.ShapeDtypeStruct((t, d), F32)] * 2,
        compiler_params=_params(("parallel",)),
        name="matmul_res",
    )(a, w, x)


def _rmsnorm_kernel(x_ref, g_ref, o_ref):
    o_ref[...] = _rmsnorm_rows(x_ref[...], g_ref[...])


def _rmsnorm(x, g, *, tm=256):
    t, d = x.shape
    tm = _pick(t, tm)
    return pl.pallas_call(
        _rmsnorm_kernel,
        grid=(t // tm,),
        in_specs=[pl.BlockSpec((tm, d), lambda i: (i, 0)),
                  pl.BlockSpec((1, d), lambda i: (0, 0))],
        out_specs=pl.BlockSpec((tm, d), lambda i: (i, 0)),
        out_shape=jax.ShapeDtypeStruct((t, d), F32),
        compiler_params=_params(("parallel",)),
        name="final_rmsnorm",
    )(x, g.reshape(1, d))


FFT_S2 = 128
FFT_STAGE2_COLS = 1024


def _dft_tables(seq, ch):
    s1n = seq // FFT_S2
    k1 = np.arange(s1n, dtype=np.float64)
    ang1 = 2.0 * np.pi * np.outer(k1, k1) / s1n
    m1 = np.concatenate([np.cos(ang1), -np.sin(ang1)], axis=0)
    s2 = np.arange(FFT_S2, dtype=np.float64)
    kk = k1[:, None, None] + s1n * s2[None, :, None]
    ang2 = 2.0 * np.pi * ((kk * s2[None, None, :]) % seq) / seq
    gc, gs = np.cos(ang2), np.sin(ang2)
    g = np.concatenate([np.concatenate([gc, gs], axis=2),
                        np.concatenate([-gs, gc], axis=2)], axis=1)
    c = np.arange(ch, dtype=np.float64)
    angc = 2.0 * np.pi * (np.outer(c, c) % ch) / ch
    cs = np.concatenate([np.cos(angc), np.sin(angc)], axis=0) / math.sqrt(seq * ch)
    return (jnp.asarray(m1, F32).astype(BF16), jnp.asarray(g, F32).astype(BF16),
            jnp.asarray(cs, F32).astype(BF16))


def _fft_stage1_kernel(m_ref, u_ref, y_ref, uf_ref, yf_ref):
    s1n, sb, tn = u_ref.shape
    chunks = [slice(c * LANES, (c + 1) * LANES) for c in range(tn // LANES)]
    for c, cs in enumerate(chunks):
        uf_ref[c] = u_ref[:, :, cs].astype(F32).reshape(s1n * sb, LANES)
    for j in range(sb):
        rows = jnp.concatenate([uf_ref[c, pl.ds(j, s1n, stride=sb), :] for c in range(len(chunks))], axis=1)
        res = jnp.dot(m_ref[...], rows.astype(BF16), preferred_element_type=F32)
        for c, cs in enumerate(chunks):
            yf_ref[c, pl.ds(j, 2 * s1n, stride=sb), :] = res[:, cs]
    for c, cs in enumerate(chunks):
        y_ref[:, :, cs] = yf_ref[c].reshape(2 * s1n, sb, LANES).astype(y_ref.dtype)


def _fft_stage1(u, m1, *, tn=1024):
    b, s1n, _, d = u.shape
    tn = _pick(d, tn)
    sb = BF16_SUBLANES
    return pl.pallas_call(
        _fft_stage1_kernel,
        grid=(b, FFT_S2 // sb, d // tn),
        in_specs=[pl.BlockSpec((2 * s1n, s1n), lambda i, j, c: (0, 0)),
                  pl.BlockSpec((None, s1n, sb, tn), lambda i, j, c: (i, 0, j, c))],
        out_specs=pl.BlockSpec((None, 2 * s1n, sb, tn), lambda i, j, c: (i, 0, j, c)),
        out_shape=jax.ShapeDtypeStruct((b, 2 * s1n, FFT_S2, d), BF16),
        scratch_shapes=[pltpu.VMEM((tn // LANES, s1n * sb, LANES), F32),
                        pltpu.VMEM((tn // LANES, 2 * s1n * sb, LANES), F32)],
        compiler_params=_params(("parallel", "parallel", "parallel")),
        name="fft_stage1",
    )(m1, u)


def _fft_stage2_kernel(g_ref, yr_ref, yi_ref, cs_ref, f_ref, ff_ref):
    kb = g_ref.shape[0]
    ch = cs_ref.shape[1]
    for j in range(kb):
        y = jnp.concatenate([yr_ref[j], yi_ref[j]], axis=0)
        xc = jnp.dot(g_ref[j], y, preferred_element_type=F32).astype(BF16)
        for q in range(y.shape[1] // ch):
            cols = slice(q * ch, (q + 1) * ch)
            xg = jnp.concatenate([xc[:FFT_S2, cols], xc[FFT_S2:, cols]], axis=1)
            ff_ref[:, j, cols] = jnp.dot(xg, cs_ref[...], preferred_element_type=F32)
    f_ref[...] = ff_ref[...].astype(f_ref.dtype)


def _fft_stage2(y, g, cs):
    b, s1n2, _, d = y.shape
    s1n = s1n2 // 2
    ch = cs.shape[1]
    kb = BF16_SUBLANES
    assert s1n % kb == 0
    nk = s1n // kb
    tc = _pick(d, FFT_STAGE2_COLS)
    assert tc % ch == 0
    return pl.pallas_call(
        _fft_stage2_kernel,
        grid=(b, nk, d // tc),
        in_specs=[pl.BlockSpec((kb, 2 * FFT_S2, 2 * FFT_S2), lambda i, j, q: (j, 0, 0)),
                  pl.BlockSpec((None, kb, FFT_S2, tc), lambda i, j, q: (i, j, 0, q)),
                  pl.BlockSpec((None, kb, FFT_S2, tc), lambda i, j, q: (i, j + nk, 0, q)),
                  pl.BlockSpec(cs.shape, lambda i, j, q: (0, 0))],
        out_specs=pl.BlockSpec((None, FFT_S2, kb, tc), lambda i, j, q: (i, 0, j, q)),
        out_shape=jax.ShapeDtypeStruct((b, FFT_S2, s1n, d), BF16),
        scratch_shapes=[pltpu.VMEM((FFT_S2, kb, tc), F32)],
        compiler_params=_params(("parallel", "parallel", "parallel")),
        name="fft_stage2",
    )(g, y, y, cs)


def _fourier_layer(x, g_norm, w_in, w_out):
    b, s, d = x.shape
    assert s % FFT_S2 == 0 and d % A_GROUPS == 0
    s1n = s // FFT_S2
    m1, g, cs = _dft_tables(s, d // A_GROUPS)
    u = _norm_matmul(x, g_norm, w_in)
    y = _fft_stage1(u.reshape(b, s1n, FFT_S2, d), m1)
    f = _fft_stage2(y, g, cs)
    return [o.reshape(b, s, d) for o in _matmul_res(f.reshape(b * s, d), w_out, x.reshape(b * s, d))]


DIL_TQ = 128
DIL_HALF = 64
DIL_HEAD_GROUP = 8
DIL_QBLOCKS = 4


def _lookup_rows(table, index):
    onehot = jnp.asarray(np.eye(table.shape[0], dtype=np.float32)[index])
    return jnp.tensordot(onehot, table.astype(F32), axes=1, precision=lax.Precision.HIGHEST)


def _t5_bucket_np(rel):
    nb = T5_BUCKETS // 2
    max_exact = nb // 2
    n = np.abs(rel)
    large = max_exact + (np.log(np.maximum(n, 1) / max_exact) / math.log(T5_MAX_DIST / max_exact)
                         * (nb - max_exact)).astype(np.int32)
    large = np.minimum(large, nb - 1)
    return (np.where(rel > 0, nb, 0) + np.where(n < max_exact, n, large)).astype(np.int32)


def _dilated_bias(t5_bias_g, dilation):
    off = np.arange(DIL_TQ + 2 * DIL_HALF)[None, :] - DIL_HALF - np.arange(DIL_TQ)[:, None]
    band = np.abs(off) <= DIL_HALF
    bias = _lookup_rows(t5_bias_g, _t5_bucket_np(off * dilation)).transpose(2, 0, 1)
    return jnp.where(band[None], bias.astype(F32), NEG_INF)


def _dilated_attn_kernel(bias_ref, q_ref, *refs, heads, sub_len, qb):
    k_refs, v_refs = refs[:qb + 2], refs[qb + 2:2 * qb + 4]
    o_ref, l_ref = refs[2 * qb + 4:]
    i = pl.program_id(2)
    nk = DIL_TQ + 2 * DIL_HALF
    lane = lax.broadcasted_iota(jnp.int32, (DIL_TQ, LANES), 1)
    scale = HEAD_DIM ** -0.5

    def window(blocks, jb, hs):
        return jnp.concatenate([blocks[jb][DIL_TQ - DIL_HALF:, hs], blocks[jb + 1][:, hs],
                                blocks[jb + 2][:DIL_HALF, hs]], axis=0)

    for jb in range(qb):
        rows = slice(jb * DIL_TQ, (jb + 1) * DIL_TQ)
        kpos = (i * qb + jb) * DIL_TQ - DIL_HALF + lax.broadcasted_iota(jnp.int32, (1, nk), 1)
        valid = (kpos >= 0) & (kpos < sub_len)
        lse_all = jnp.zeros((DIL_TQ, LANES), F32)
        for h0 in range(0, heads, DIL_HEAD_GROUP):
            hss = [slice(h * HEAD_DIM, (h + 1) * HEAD_DIM) for h in range(h0, h0 + DIL_HEAD_GROUP)]
            ss = [lax.dot_general(q_ref[rows, hs], window(k_refs, jb, hs), (((1,), (1,)), ((), ())),
                                  preferred_element_type=F32) for hs in hss]
            ps = []
            for j, s in enumerate(ss):
                s = jnp.where(valid, s * scale + bias_ref[h0 + j], NEG_INF)
                m = jnp.max(s, axis=-1, keepdims=True)
                p = jnp.exp(s - m)
                den = jnp.sum(p, axis=-1, keepdims=True)
                ps.append((p * (1.0 / den)).astype(BF16))
                lse_all = jnp.where(lane == h0 + j, m + jnp.log(den), lse_all)
            for p, hs in zip(ps, hss):
                o = jnp.dot(p, window(v_refs, jb, hs), preferred_element_type=F32)
                o_ref[rows, hs] = o.astype(o_ref.dtype)
        l_ref[rows, :] = lse_all


def _dilated_attn(qkv, bias, dilation, heads):
    b, r, sub_len, _ = qkv.shape
    hw = heads * HEAD_DIM
    assert r == dilation and sub_len % DIL_TQ == 0
    nb = sub_len // DIL_TQ
    qb = _pick(nb, DIL_QBLOCKS)
    rows = qb * DIL_TQ

    def kv_spec(c, j):
        return pl.BlockSpec((None, None, DIL_TQ, hw),
                            lambda bi, rho, i: (bi, rho, jnp.clip(i * qb + j - 1, 0, nb - 1), c))

    def row_spec(w, c):
        return pl.BlockSpec((None, None, rows, w), lambda bi, rho, i: (bi, rho, i, c))

    return pl.pallas_call(
        functools.partial(_dilated_attn_kernel, heads=heads, sub_len=sub_len, qb=qb),
        grid=(b, r, nb // qb),
        in_specs=[pl.BlockSpec(bias.shape, lambda bi, rho, i: (0, 0, 0)), row_spec(hw, 0)]
                 + [kv_spec(1, j) for j in range(qb + 2)] + [kv_spec(2, j) for j in range(qb + 2)],
        out_specs=[row_spec(hw, 0), row_spec(LANES, 0)],
        out_shape=[jax.ShapeDtypeStruct((b, r, sub_len, hw), BF16),
                   jax.ShapeDtypeStruct((b, r, sub_len, LANES), F32)],
        compiler_params=_params(("parallel", "parallel", "parallel")),
        name=f"dilated_attn_r{dilation}",
    )(bias, *([qkv] * (2 * qb + 5)))


def _dilated_combine_kernel(o0_ref, o1_ref, o2_ref, l0_ref, l1_ref, l2_ref, w_ref, x_ref, out_ref, out2_ref,
                            a_ref, oi_ref, li_ref, *, heads):
    for g, (o_ref, l_ref) in enumerate(((o0_ref, l0_ref), (o1_ref, l1_ref), (o2_ref, l2_ref))):
        r, per = o_ref.shape[0], o_ref.shape[1]
        for rho in range(r):
            rows = pl.ds(rho, per, stride=r) if r > 1 else slice(None)
            for h in range(heads):
                oi_ref[g, h, rows, :] = o_ref[rho, :, h * HEAD_DIM:(h + 1) * HEAD_DIM].astype(F32)
            li_ref[g, rows, :] = l_ref[rho]
    l0, l1, l2 = li_ref[0], li_ref[1], li_ref[2]
    m = jnp.maximum(jnp.maximum(l0, l1), l2)
    e0, e1, e2 = jnp.exp(l0 - m), jnp.exp(l1 - m), jnp.exp(l2 - m)
    inv = 1.0 / (e0 + e1 + e2)
    a0, a1, a2 = e0 * inv, e1 * inv, e2 * inv
    for h in range(heads):
        hs = slice(h * HEAD_DIM, (h + 1) * HEAD_DIM)
        o = a0[:, h:h + 1] * oi_ref[0, h] + a1[:, h:h + 1] * oi_ref[1, h] + a2[:, h:h + 1] * oi_ref[2, h]
        a_ref[:, hs] = o.astype(BF16)
    res = x_ref[...] + jnp.dot(a_ref[...], w_ref[...], preferred_element_type=F32)
    out_ref[...] = res
    out2_ref[...] = res


def _dilated_combine(os, ls, w_out, x, *, tm=512):
    b, s, d = x.shape
    hw = w_out.shape[0]
    tm = _pick(s, tm)
    per_b = s // tm

    def dspec(a):
        r, w = a.shape[1], a.shape[3]
        assert tm % (r * BF16_SUBLANES) == 0
        return pl.BlockSpec((None, r, tm // r, w), lambda i: (i // per_b, 0, i % per_b, 0))

    xspec = pl.BlockSpec((None, tm, d), lambda i: (i // per_b, i % per_b, 0))
    return pl.pallas_call(
        functools.partial(_dilated_combine_kernel, heads=hw // HEAD_DIM),
        grid=(b * per_b,),
        in_specs=[dspec(a) for a in os] + [dspec(a) for a in ls] + [pl.BlockSpec((hw, d), lambda i: (0, 0)), xspec],
        out_specs=[xspec, xspec],
        out_shape=[jax.ShapeDtypeStruct((b, s, d), F32)] * 2,
        scratch_shapes=[pltpu.VMEM((tm, hw), BF16), pltpu.VMEM((3, hw // HEAD_DIM, tm, HEAD_DIM), F32),
                        pltpu.VMEM((3, tm, LANES), F32)],
        compiler_params=_params(("parallel",)),
        name="dilated_combine",
    )(*os, *ls, w_out, x)


def _dilated_layer(x, g_norm, w_qkv, w_out, t5_bias):
    hw = w_out.shape[0]
    heads = hw // HEAD_DIM
    assert len(B_PAIRS) == 3 and w_qkv.shape[1] == 9 * hw
    os, ls = [], []
    for g, (window, dilation) in enumerate(B_PAIRS):
        assert window // (2 * dilation) == DIL_HALF
        qkv = _norm_matmul(x, g_norm, w_qkv, col0=g * 3 * hw, n=3 * hw, dilation=dilation)
        bias = _dilated_bias(t5_bias[:, g * heads:(g + 1) * heads], dilation)
        o, l = _dilated_attn(qkv, bias, dilation, heads)
        os.append(o)
        ls.append(l)
    return _dilated_combine(os, ls, w_out, x)


NA_QROWS = 8
NA_KBLK = 4
NA_WIN = NA_QROWS + NA_ROWS
NA_HEAD_GROUP = 8


def _na_bias_table(rpb):
    scale = HEAD_DIM ** -0.5
    qc = np.arange(GRID_W)[None, :]
    kc = np.arange(GRID_W)[:, None]
    cstart = np.clip(qc - NA_COLS // 2, 0, GRID_W - NA_COLS)
    ok = (kc >= cstart) & (kc < cstart + NA_COLS)
    dcol = np.clip(kc - qc, -(NA_COLS - 1), NA_COLS - 1) + NA_COLS - 1
    tab = _lookup_rows(jnp.moveaxis(rpb.astype(F32), 2, 0), dcol)
    tab = jnp.moveaxis(tab, (2, 3), (0, 1))
    tab = jnp.where(ok[None, None], tab, NEG_INF) / scale
    hi = tab.astype(BF16)
    lo = (tab - hi.astype(F32)).astype(BF16)
    out = jnp.concatenate([hi, lo], axis=-1)
    return out.reshape(rpb.shape[0], (2 * NA_ROWS - 1) * GRID_W, 2 * GRID_W)


def _na_attn_kernel(bk_ref, q_ref, k0_ref, k1_ref, k2_ref, k3_ref, v0_ref, v1_ref, v2_ref, v3_ref,
                    o_ref, kw_ref, vw_ref, bias_ref, *, heads, rows):
    rb = pl.program_id(1)
    blk = NA_KBLK * GRID_W
    for j, (kr, vr) in enumerate(((k0_ref, v0_ref), (k1_ref, v1_ref), (k2_ref, v2_ref), (k3_ref, v3_ref))):
        kw_ref[j * blk:(j + 1) * blk, :] = kr[...]
        vw_ref[j * blk:(j + 1) * blk, :] = vr[...]
    ws = NA_KBLK * jnp.clip(2 * rb - 1, 0, rows // NA_KBLK - NA_WIN // NA_KBLK)
    nkeys = NA_ROWS * GRID_W
    eye = (lax.broadcasted_iota(jnp.int32, (GRID_W, 2 * GRID_W), 0)
           == lax.broadcasted_iota(jnp.int32, (GRID_W, 2 * GRID_W), 1) % GRID_W)
    eye2 = jnp.where(eye, 1.0, 0.0).astype(BF16)
    scale = HEAD_DIM ** -0.5
    nt = (((1,), (1,)), ((), ()))

    def bias_of(h, bsl):
        return lax.dot_general(eye2, bk_ref[h, bsl, :], nt, preferred_element_type=F32)

    def row_loop(bias_is_staged):
        def row_body(rr, carry):
            r = rb * NA_QROWS + rr
            rs = jnp.clip(r - NA_ROWS // 2, 0, rows - NA_ROWS)
            ksl = pl.ds(pl.multiple_of((rs - ws) * GRID_W, GRID_W), nkeys)
            bsl = pl.ds(pl.multiple_of((rs - r + NA_ROWS - 1) * GRID_W, GRID_W), nkeys)
            qsl = pl.ds(pl.multiple_of(rr * GRID_W, GRID_W), GRID_W)
            for h0 in range(0, heads, NA_HEAD_GROUP):
                hss = [slice(h * HEAD_DIM, (h + 1) * HEAD_DIM) for h in range(h0, h0 + NA_HEAD_GROUP)]
                ss = [lax.dot_general(q_ref[qsl, hs], kw_ref[ksl, hs], nt, preferred_element_type=F32)
                      + (bias_ref[h0 + j] if bias_is_staged else bias_of(h0 + j, bsl))
                      for j, hs in enumerate(hss)]
                ps = []
                for s in ss:
                    s = s * scale
                    p = jnp.exp(s - jnp.max(s, axis=-1, keepdims=True))
                    ps.append((p * (1.0 / jnp.sum(p, axis=-1, keepdims=True))).astype(BF16))
                for p, hs in zip(ps, hss):
                    o_ref[qsl, hs] = jnp.dot(p, vw_ref[ksl, hs], preferred_element_type=F32).astype(o_ref.dtype)
            return carry
        lax.fori_loop(0, NA_QROWS, row_body, 0)

    interior = (rb > 0) & (rb < pl.num_programs(1) - 1)

    @pl.when(interior)
    def _():
        centre = pl.ds((NA_ROWS - 1 - NA_ROWS // 2) * GRID_W, nkeys)
        for h in range(heads):
            bias_ref[h] = bias_of(h, centre)
        row_loop(True)

    @pl.when(jnp.logical_not(interior))
    def _():
        row_loop(False)


def _na_attn(qkv, bk, heads):
    b, s, _ = qkv.shape
    hw = heads * HEAD_DIM
    rows = s // GRID_W
    assert s % GRID_W == 0 and rows % NA_QROWS == 0 and rows >= NA_WIN and 2 * GRID_W == LANES
    assert heads % NA_HEAD_GROUP == 0
    nwb = rows // NA_KBLK - NA_WIN // NA_KBLK
    blk = NA_KBLK * GRID_W

    def kv_spec(c, j):
        return pl.BlockSpec((None, blk, hw), lambda bi, rb: (bi, jnp.clip(2 * rb - 1, 0, nwb) + j, c))

    o = pl.pallas_call(
        functools.partial(_na_attn_kernel, heads=heads, rows=rows),
        grid=(b, rows // NA_QROWS),
        in_specs=[pl.BlockSpec(bk.shape, lambda bi, rb: (0, 0, 0)),
                  pl.BlockSpec((None, NA_QROWS * GRID_W, hw), lambda bi, rb: (bi, rb, 0))]
                 + [kv_spec(1, j) for j in range(4)] + [kv_spec(2, j) for j in range(4)],
        out_specs=pl.BlockSpec((None, NA_QROWS * GRID_W, hw), lambda bi, rb: (bi, rb, 0)),
        out_shape=jax.ShapeDtypeStruct((b, s, hw), BF16),
        scratch_shapes=[pltpu.VMEM((NA_WIN * GRID_W, hw), BF16), pltpu.VMEM((NA_WIN * GRID_W, hw), BF16),
                        pltpu.VMEM((heads, GRID_W, NA_ROWS * GRID_W), F32)],
        compiler_params=_params(("parallel", "parallel")),
        name="na_attn",
    )(bk, *([qkv] * 9))
    return o.reshape(b * s, hw)


def _neighbourhood_layer(x, g_norm, w_qkv, w_out, rpb):
    b, s, d = x.shape
    hw = w_out.shape[0]
    assert w_qkv.shape[1] == 3 * hw
    qkv = _norm_matmul(x, g_norm, w_qkv).reshape(b, s, 3 * hw)
    o = _na_attn(qkv, _na_bias_table(rpb), hw // HEAD_DIM)
    return [y.reshape(b, s, d) for y in _matmul_res(o, w_out, x.reshape(b * s, d))]


def _router_kernel(x_ref, g_ref, wt_ref, aff_ref):
    hn = _rmsnorm_rows(x_ref[...], g_ref[...]).astype(BF16)
    logits = lax.dot_general(wt_ref[...], hn, (((1,), (1,)), ((), ())), preferred_element_type=F32)
    m = jnp.max(logits, axis=0, keepdims=True)
    p = jnp.exp(logits - m)
    aff_ref[...] = p / jnp.sum(p, axis=0, keepdims=True)


def _router(x, g, w_router_t, *, tm=512):
    t, d = x.shape
    e = w_router_t.shape[0]
    tm = _pick(t, tm)
    return pl.pallas_call(
        _router_kernel,
        grid=(t // tm,),
        in_specs=[pl.BlockSpec((tm, d), lambda i: (i, 0)),
                  pl.BlockSpec((1, d), lambda i: (0, 0)),
                  pl.BlockSpec((e, d), lambda i: (0, 0))],
        out_specs=pl.BlockSpec((e, tm), lambda i: (0, i)),
        out_shape=jax.ShapeDtypeStruct((e, t), F32),
        compiler_params=_params(("parallel",)),
        name="moe_router",
    )(x, g.reshape(1, d), w_router_t)


def _topk_mask_kernel(aff_ref, sel_ref, *, cap):
    bits = pltpu.bitcast(aff_ref[...], jnp.int32)
    e, t = bits.shape

    def count(mask):
        return jnp.sum(jnp.where(mask, 1, 0), axis=1, keepdims=True)

    def value_step(_, c):
        lo, hi = c
        mid = lo + (hi - lo) // 2
        ok = count(bits >= mid) >= cap
        return jnp.where(ok, mid, lo), jnp.where(ok, hi, mid)

    lo0 = jnp.zeros((e, 1), jnp.int32)
    hi0 = jnp.full((e, 1), 0x7F800000, jnp.int32)
    thr, _ = lax.fori_loop(0, 31, value_step, (lo0, hi0))
    above = bits > thr
    tie = bits == thr
    need = cap - count(above)
    tok = lax.broadcasted_iota(jnp.int32, (e, t), 1)

    def index_step(_, c):
        lo, hi = c
        mid = lo + (hi - lo) // 2
        ok = count(tie & (tok < mid)) >= need
        return jnp.where(ok, lo, mid), jnp.where(ok, mid, hi)

    _, bound = lax.fori_loop(0, max(1, math.ceil(math.log2(t))), index_step,
                             (jnp.zeros((e, 1), jnp.int32), jnp.full((e, 1), t, jnp.int32)))
    sel_ref[...] = jnp.where(above | (tie & (tok < bound)), 1, 0)


def _topk_mask(aff_t, cap):
    e, t = aff_t.shape
    return pl.pallas_call(
        functools.partial(_topk_mask_kernel, cap=cap),
        grid=(1,),
        in_specs=[pl.BlockSpec((e, t), lambda i: (0, 0))],
        out_specs=pl.BlockSpec((e, t), lambda i: (0, 0)),
        out_shape=jax.ShapeDtypeStruct((e, t), jnp.int32),
        compiler_params=_params(("arbitrary",)),
        name="moe_topk_mask",
    )(aff_t)


def _compact_kernel(sel_ref, aff_ref, idx_ref, gate_ref, cnt_ref, bound_ref, piece_ref, *, tj):
    nb = sel_ref.shape[0]

    @pl.when(pl.program_id(1) == 0)
    def _():
        m = sel_ref[...].astype(F32).astype(BF16)
        tri = jnp.where(lax.broadcasted_iota(jnp.int32, (LANES, LANES), 0)
                        <= lax.broadcasted_iota(jnp.int32, (LANES, LANES), 1), 1.0, 0.0).astype(BF16)
        cnt_ref[...] = jnp.dot(m, tri, preferred_element_type=F32).astype(BF16)
        tot = lax.dot_general(jnp.ones((8, LANES), BF16), m, (((1,), (1,)), ((), ())),
                              preferred_element_type=F32)
        trib = jnp.where(lax.broadcasted_iota(jnp.int32, (nb, nb), 0)
                         <= lax.broadcasted_iota(jnp.int32, (nb, nb), 1), 1.0, 0.0).astype(BF16)
        incl = jnp.dot(tot.astype(BF16), trib, preferred_element_type=F32)
        bound_ref[0:8, :] = incl
        bound_ref[8:16, :] = incl - tot
        bits = pltpu.bitcast(aff_ref[...], jnp.int32)
        for k in range(4):
            piece_ref[k] = ((bits >> (8 * k)) & 0xFF).astype(F32).astype(BF16)

    jf = (pl.program_id(1) * tj + lax.broadcasted_iota(jnp.int32, (tj, 1), 0)).astype(F32)
    blk = jnp.sum(jnp.where(bound_ref[0:1, :] <= jf, 1, 0), axis=1, keepdims=True)
    onehot = lax.broadcasted_iota(jnp.int32, (tj, nb), 1) == blk
    start = jnp.sum(jnp.where(onehot, bound_ref[8:9, :], 0.0), axis=1, keepdims=True)
    oh = jnp.where(onehot, 1.0, 0.0).astype(BF16)
    cnt = jnp.dot(oh, cnt_ref[...], preferred_element_type=F32)
    lane = jnp.sum(jnp.where(cnt < jf - start + 1.0, 1, 0), axis=1, keepdims=True)
    idx_ref[...] = blk * LANES + lane
    pick = lax.broadcasted_iota(jnp.int32, (tj, LANES), 1) == lane
    bits = jnp.zeros((tj, 1), jnp.int32)
    for k in range(4):
        byte = jnp.dot(oh, piece_ref[k], preferred_element_type=F32)
        bits = bits | (jnp.sum(jnp.where(pick, byte, 0.0), axis=1, keepdims=True).astype(jnp.int32) << (8 * k))
    gate_ref[...] = pltpu.bitcast(bits, F32)


def _compact(sel, aff_t, cap, *, tj=512):
    e, t = sel.shape
    assert t % LANES == 0
    nb = t // LANES
    tj = _pick(cap, tj)
    nj = cap // tj
    blocks = pl.BlockSpec((None, nb, LANES), lambda ei, j: (ei, 0, 0))
    col = pl.BlockSpec((tj, 1), lambda ei, j: (ei * nj + j, 0))
    return pl.pallas_call(
        functools.partial(_compact_kernel, tj=tj),
        grid=(e, nj),
        in_specs=[blocks, blocks],
        out_specs=[col, col],
        out_shape=[jax.ShapeDtypeStruct((e * cap, 1), jnp.int32), jax.ShapeDtypeStruct((e * cap, 1), F32)],
        scratch_shapes=[pltpu.VMEM((nb, LANES), BF16), pltpu.VMEM((16, nb), F32), pltpu.VMEM((4, nb, LANES), BF16)],
        compiler_params=_params(("parallel", "arbitrary")),
        name="moe_compact",
    )(sel.reshape(e, nb, LANES), aff_t.reshape(e, nb, LANES))


X_SEM, OUT_SEM, SCATTER_SEM = 0, 1, 2


def _moe_ffn_kernel(idx_ref, gate_ref, gn_ref, wg_ref, wu_ref, wd_ref, x_hbm, yin_hbm, out_hbm,
                    xg_ref, ob_ref, h_ref, acc_ref, sem, *, tm, nt, nf, n_tiles):
    del yin_hbm
    i = pl.program_id(1)
    tile = pl.program_id(0) * nt + i
    base = tile * tm
    f = pl.program_id(2)
    slot = tile % 2
    first_of_expert = i == 0
    after_first = i == 1 % nt

    def row_copy(hbm, buf, first, r, s, to_vmem):
        tok = idx_ref[first + r]
        src, dst = hbm.at[pl.ds(tok, 1), :], buf.at[pl.ds(r, 1), :]
        return (pltpu.make_async_copy(src, dst, sem.at[s]) if to_vmem
                else pltpu.make_async_copy(dst, src, sem.at[s]))

    def start_rows(hbm, buf, first, s, to_vmem):
        def body(r, c):
            row_copy(hbm, buf, first, r, s, to_vmem).start()
            return c
        lax.fori_loop(0, tm, body, 0, unroll=32)

    def wait_rows(hbm, buf, s, to_vmem):
        whole = hbm.at[pl.ds(0, tm), :]
        (pltpu.make_async_copy(whole, buf, sem.at[s]) if to_vmem
         else pltpu.make_async_copy(buf, whole, sem.at[s])).wait()

    def wait_scatter(s):
        wait_rows(out_hbm, ob_ref.at[s], SCATTER_SEM + s, False)

    @pl.when(f == 0)
    def _():
        @pl.when(tile == 0)
        def _():
            start_rows(x_hbm, xg_ref, base, X_SEM, True)

        @pl.when((tile >= 2) & jnp.logical_not(after_first))
        def _():
            wait_scatter(slot)

        @pl.when((tile >= 1) & first_of_expert)
        def _():
            wait_scatter(1 - slot)

        wait_rows(x_hbm, xg_ref, X_SEM, True)
        h_ref[...] = _rmsnorm_rows(xg_ref[...], gn_ref[...]).astype(BF16)
        acc_ref[...] = jnp.zeros_like(acc_ref)

    nxt = jnp.minimum(tile + 1, n_tiles - 1) * tm
    per_step = tm // nf
    for r in range(per_step):
        row = f * per_step + r
        row_copy(out_hbm, ob_ref.at[slot], base, row, OUT_SEM, True).start()
        row_copy(x_hbm, xg_ref, nxt, row, X_SEM, True).start()

    h = h_ref[...]
    g = jnp.dot(h, wg_ref[...].astype(BF16), preferred_element_type=F32)
    u = jnp.dot(h, wu_ref[...].astype(BF16), preferred_element_type=F32)
    hid = (g * (1.0 / (1.0 + jnp.exp(-g))) * u).astype(BF16)
    acc_ref[...] += jnp.dot(hid, wd_ref[...].astype(BF16), preferred_element_type=F32)

    @pl.when(f == nf - 1)
    def _():
        wait_rows(out_hbm, ob_ref.at[slot], OUT_SEM, True)
        ob_ref[slot] = ob_ref[slot] + acc_ref[...] * gate_ref[...]
        start_rows(out_hbm, ob_ref.at[slot], base, SCATTER_SEM + slot, False)

        @pl.when(tile == n_tiles - 1)
        def _():
            @pl.when((tile >= 1) & jnp.logical_not(first_of_expert))
            def _():
                wait_scatter(1 - slot)

            wait_scatter(slot)
            wait_rows(x_hbm, xg_ref, X_SEM, True)


def _moe_ffn(x, x_copy, idx, gate, n_experts, g_norm, wg, wu, wd, layer, *, tm=1024, tf=256):
    t, d = x.shape
    e = n_experts
    cap = idx.shape[0] // e
    fdim = wg.shape[3]
    tm = _pick(cap, tm)
    tf = _pick(fdim, tf)
    nt, nf = cap // tm, fdim // tf
    assert tm % nf == 0
    grid_spec = pltpu.PrefetchScalarGridSpec(
        num_scalar_prefetch=1,
        grid=(e, nt, nf),
        in_specs=[pl.BlockSpec((tm, 1), lambda ei, i, f, idx: (ei * nt + i, 0)),
                  pl.BlockSpec((1, d), lambda ei, i, f, idx: (0, 0)),
                  pl.BlockSpec((None, None, d, tf), lambda ei, i, f, idx: (layer, ei, 0, f)),
                  pl.BlockSpec((None, None, d, tf), lambda ei, i, f, idx: (layer, ei, 0, f)),
                  pl.BlockSpec((None, None, tf, d), lambda ei, i, f, idx: (layer, ei, f, 0)),
                  pl.BlockSpec(memory_space=pl.ANY),
                  pl.BlockSpec(memory_space=pl.ANY)],
        out_specs=pl.BlockSpec(memory_space=pl.ANY),
        scratch_shapes=[pltpu.VMEM((tm, d), F32), pltpu.VMEM((2, tm, d), F32), pltpu.VMEM((tm, d), BF16),
                        pltpu.VMEM((tm, d), F32), pltpu.SemaphoreType.DMA((4,))],
    )
    return pl.pallas_call(
        functools.partial(_moe_ffn_kernel, tm=tm, nt=nt, nf=nf, n_tiles=e * nt),
        grid_spec=grid_spec,
        out_shape=jax.ShapeDtypeStruct((t, d), F32),
        input_output_aliases={7: 0},
        compiler_params=_params(("arbitrary", "arbitrary", "arbitrary")),
        name="moe_ffn",
    )(idx.reshape(e * cap), gate, g_norm.reshape(1, d), wg, wu, wd, x, x_copy)


def _moe_layer(x, x_copy, g_norm, w_router, wg, wu, wd, layer):
    b, s, d = x.shape
    t = b * s
    e = w_router.shape[1]
    cap = EC_CAPACITY * t // e
    xt = x.reshape(t, d)
    aff_t = _router(xt, g_norm, w_router.T.astype(BF16))
    sel = _topk_mask(aff_t, cap)
    idx, gate = _compact(sel, aff_t, cap)
    return _moe_ffn(xt, x_copy.reshape(t, d), idx, gate, e, g_norm, wg, wu, wd, layer).reshape(b, s, d)


def _trunk(x, w):
    for i in range(w["norm_mix"].shape[0]):
        kind, j = i % N_MIXERS, i // N_MIXERS
        if kind == 0:
            x, x2 = _fourier_layer(x, w["norm_mix"][i], w["a_w_in"][j], w["a_w_out"][j])
        elif kind == 1:
            x, x2 = _dilated_layer(x, w["norm_mix"][i], w["b_w_qkv"][j], w["b_w_out"][j], w["t5_bias"])
        else:
            x, x2 = _neighbourhood_layer(x, w["norm_mix"][i], w["c_w_qkv"][j], w["c_w_out"][j], w["c_rpb"][j])
        x = _moe_layer(x, x2, w["norm_ffn"][i], w["moe_router"][i], w["moe_w_gate"], w["moe_w_up"],
                       w["moe_w_down"], i)
    b, s, d = x.shape
    return _rmsnorm(x.reshape(b * s, d), w["norm_final"]).reshape(b, s, d)


def kernel(x_prompt, x_sample, norm_mix, norm_ffn, norm_final, a_w_in, a_w_out, b_w_qkv, b_w_out, t5_bias,
           c_w_qkv, c_w_out, c_rpb, moe_router, moe_w_gate, moe_w_up, moe_w_down):
    bf = lambda a: a.astype(BF16)
    w = dict(norm_mix=norm_mix, norm_ffn=norm_ffn, norm_final=norm_final, a_w_in=bf(a_w_in), a_w_out=bf(a_w_out),
             b_w_qkv=bf(b_w_qkv), b_w_out=bf(b_w_out), t5_bias=t5_bias, c_w_qkv=bf(c_w_qkv), c_w_out=bf(c_w_out),
             c_rpb=c_rpb, moe_router=moe_router, moe_w_gate=moe_w_gate, moe_w_up=moe_w_up, moe_w_down=moe_w_down)
    return _trunk(x_prompt, w), _trunk(x_sample, w)
```
